```python
import math
import jax
import jax.numpy as jnp
from jax import lax
import numpy as np

D_MODEL = 1024
BATCH = 16
SEQ = 2048
DEPTH = 4
DEC_BATCH = 128
DEC_SEQ = 8
PAST_LEN = 8192
PAGE_SIZE = 128

N_ATTN_LAYERS = (DEPTH + 1) // 2
N_GLA_LAYERS = DEPTH // 2
MLA_HEADS = 8
MLA_NOPE = 64
MLA_ROPE = 32
MLA_V = 64
Q_LORA = 256
KV_LORA = 128
DIFF_HEADS = 4
DIFF_DH = 64
DIFF_V = 2 * DIFF_DH
GLA_HEADS = 4
GLA_DK = D_MODEL // 2 // GLA_HEADS
GLA_DV = D_MODEL // GLA_HEADS
GLA_GATE_RANK = 16
GLA_GATE_TAU = 16.0
GLA_CHUNK = 64
D_FF = 2816
CONV_W = 3

ROPE_THETA = 10000.0
Q_BLOCK = 128
NORM_EPS = 1e-6
NEG_INF = -1e30
POOL_NUM = 5
POOL_DEN = 4

ATTN_SPLIT = (Q_LORA, KV_LORA, MLA_ROPE, DIFF_HEADS * 2 * DIFF_DH, DIFF_HEADS * 2 * DIFF_DH, DIFF_HEADS * DIFF_V)
ATTN_IN = Q_LORA + KV_LORA + MLA_ROPE + 2 * DIFF_HEADS * 2 * DIFF_DH + DIFF_HEADS * DIFF_V
ATTN_OUT = MLA_HEADS * MLA_V + DIFF_HEADS * DIFF_V
GLA_SPLIT = (GLA_HEADS * GLA_DK, GLA_HEADS * GLA_DK, GLA_HEADS * GLA_DV, GLA_GATE_RANK, GLA_HEADS * GLA_DV)
GLA_IN = 2 * GLA_HEADS * GLA_DK + 2 * GLA_HEADS * GLA_DV + GLA_GATE_RANK
MLA_SCALE = (MLA_NOPE + MLA_ROPE) ** -0.5
DIFF_SCALE = DIFF_DH ** -0.5

kernel_name = 'mla_diffattn_gla_convglu_decode_step'


def _cuts(sizes):
    out, acc = [], 0
    for s in sizes[:-1]:
        acc += s
        out.append(acc)
    return out


def rms_norm(x, g):
    xf = x.astype(jnp.float32)
    y = xf * lax.rsqrt(jnp.mean(xf * xf, axis=-1, keepdims=True) + NORM_EPS)
    return (y * g.astype(jnp.float32)).astype(x.dtype)


def rope(x, pos):
    d = x.shape[-1]
    half = d // 2
    inv_freq = ROPE_THETA ** (-jnp.arange(half, dtype=jnp.float32) * (2.0 / d))
    ang = pos.astype(jnp.float32)[:, None] * inv_freq
    shp = (ang.shape[0],) + (1,) * (x.ndim - 3) + (half,)
    cos = jnp.cos(ang).reshape(shp)
    sin = jnp.sin(ang).reshape(shp)
    xf = x.astype(jnp.float32)
    x1, x2 = xf[..., :half], xf[..., half:]
    return jnp.concatenate([x1 * cos - x2 * sin, x2 * cos + x1 * sin], axis=-1).astype(x.dtype)


def masked_softmax(s, mask):
    return jax.nn.softmax(jnp.where(mask, s.astype(jnp.float32), NEG_INF), axis=-1)


def attn_project(h, pos, w_in, g_qn, w_uq, g_kvn, w_uk):
    B, T, _ = h.shape
    cq, ckv, kr, dq, dk, dv = jnp.split(h @ w_in, _cuts(ATTN_SPLIT), axis=-1)
    q = (rms_norm(cq, g_qn) @ w_uq).reshape(B, T, MLA_HEADS, MLA_NOPE + MLA_ROPE)
    q_lat = jnp.einsum('bthn,chn->bthc', q[..., :MLA_NOPE], w_uk)
    q_rope = rope(q[..., MLA_NOPE:], pos)
    ckv = rms_norm(ckv, g_kvn)
    kr = rope(kr[:, :, None, :], pos)[:, :, 0, :]
    dq = rope(dq.reshape(B, T, DIFF_HEADS, 2, DIFF_DH), pos)
    dk = rope(dk.reshape(B, T, DIFF_HEADS, 2, DIFF_DH), pos)
    dv = dv.reshape(B, T, DIFF_HEADS, DIFF_V)
    return q_lat, q_rope, dq, ckv, kr, dk, dv


def attend_block(q_lat, q_rope, dq, ckv, kr, dk, dv, mask, lam):
    s_mla = (jnp.einsum('bqhc,bkc->bhqk', q_lat, ckv) + jnp.einsum('bqhr,bkr->bhqk', q_rope, kr)) * MLA_SCALE
    p = masked_softmax(s_mla, mask)
    o_lat = jnp.einsum('bhqk,bkc->bqhc', p.astype(ckv.dtype), ckv)
    s_diff = jnp.einsum('bqhid,bkhid->bhiqk', dq, dk) * DIFF_SCALE
    p_d = masked_softmax(s_diff, mask)
    w = p_d[:, :, 0] - lam * p_d[:, :, 1]
    o_diff = jnp.einsum('bhqk,bkhv->bqhv', w.astype(dv.dtype), dv)
    return o_lat, o_diff


def prompt_attention(q_lat, q_rope, dq, ckv, kr, dk, dv, lam):
    B, S = q_lat.shape[:2]
    nblk = S // Q_BLOCK
    kpos = jnp.arange(S)

    def blockify(a):
        return jnp.moveaxis(a.reshape((B, nblk, Q_BLOCK) + a.shape[2:]), 1, 0)

    def one(args):
        i, ql, qr, qd = args
        qpos = i * Q_BLOCK + jnp.arange(Q_BLOCK)
        mask = kpos[None, :] <= qpos[:, None]
        return attend_block(ql, qr, qd, ckv, kr, dk, dv, mask, lam)

    o_lat, o_diff = lax.map(one, (jnp.arange(nblk), blockify(q_lat), blockify(q_rope), blockify(dq)))

    def unblock(o):
        return jnp.moveaxis(o, 0, 1).reshape((B, S) + o.shape[3:])

    return unblock(o_lat), unblock(o_diff)


def sample_attention(q_lat, q_rope, dq, ckv, kr, dk, dv, lam, lat_pool, rope_pool, k_pool, v_pool, page_table):
    T = q_lat.shape[1]
    past = page_table.shape[1] * PAGE_SIZE
    qpos = past + jnp.arange(T)
    kpos = jnp.arange(past + T)
    mask = kpos[None, :] <= qpos[:, None]

    def gather(pool, pt, new):
        g = pool[pt].reshape((past,) + pool.shape[2:]).astype(new.dtype)
        return jnp.concatenate([g, new], axis=0)[None]

    def one(args):
        pt, ql, qr, qd, c_new, r_new, k_new, v_new = args
        o_lat, o_diff = attend_block(ql[None], qr[None], qd[None],
                                     gather(lat_pool, pt, c_new), gather(rope_pool, pt, r_new),
                                     gather(k_pool, pt, k_new), gather(v_pool, pt, v_new), mask, lam)
        return o_lat[0], o_diff[0]

    return lax.map(one, (page_table, q_lat, q_rope, dq, ckv, kr, dk, dv))


def gla_chunked(q, k, v, log_a, S0):
    B, T, H, DK = q.shape
    DV = v.shape[-1]
    C = math.gcd(T, GLA_CHUNK)
    N = T // C
    causal = jnp.tril(jnp.ones((C, C), dtype=bool))[None, :, :, None, None]

    def chunks(a):
        return jnp.moveaxis(a.astype(jnp.float32).reshape((B, N, C) + a.shape[2:]), 1, 0)

    def step(S, xs):
        qc, kc, vc, gc = xs
        b = jnp.cumsum(gc, axis=1)
        o_inter = jnp.einsum('bthk,bhkv->bthv', qc * jnp.exp(b), S)
        rel = jnp.where(causal, b[:, :, None] - b[:, None, :], -jnp.inf)
        scores = jnp.einsum('bthk,bshk,btshk->bhts', qc, kc, jnp.exp(rel))
        o_intra = jnp.einsum('bhts,bshv->bthv', scores, vc)
        b_end = b[:, -1]
        S = jnp.exp(b_end)[..., None] * S + jnp.einsum('bshk,bshv->bhkv', kc * jnp.exp(b_end[:, None] - b), vc)
        return S, o_inter + o_intra

    S, o = lax.scan(step, S0.astype(jnp.float32), (chunks(q), chunks(k), chunks(v), chunks(log_a)))
    return jnp.moveaxis(o, 0, 1).reshape(B, T, H, DV), S


def gla_mixer(h, S0, w_in, w_gate, b_gate, g_out, w_out):
    B, T, _ = h.shape
    q, k, v, a, r = jnp.split(h @ w_in, _cuts(GLA_SPLIT), axis=-1)
    q = q.reshape(B, T, GLA_HEADS, GLA_DK) * (GLA_DK ** -0.5)
    k = k.reshape(B, T, GLA_HEADS, GLA_DK)
    v = v.reshape(B, T, GLA_HEADS, GLA_DV)
    log_a = (jax.nn.log_sigmoid((a @ w_gate + b_gate).astype(jnp.float32)) / GLA_GATE_TAU).reshape(B, T, GLA_HEADS, GLA_DK)
    o, S = gla_chunked(q, k, v, log_a, S0)
    o = rms_norm(o.astype(h.dtype), g_out).reshape(B, T, GLA_HEADS * GLA_DV) * jax.nn.silu(r)
    return o @ w_out, S


def conv_ffn(h, prev, w_up, w_conv, b_conv, w_down):
    T = h.shape[1]
    a, v = jnp.split(h @ w_up, 2, axis=-1)
    ext = jnp.concatenate([prev.astype(a.dtype), a], axis=1)
    a_c = b_conv
    for j in range(CONV_W):
        a_c = a_c + w_conv[j] * ext[:, j:j + T]
    y = (jax.nn.gelu(a_c, approximate=False) * v) @ w_down
    return y, ext[:, T:]


def run_trunk(x, pos, attend, gla_state0, conv_state0, W):
    B, T, _ = x.shape
    lat, kro, kd, vd, gla, conv = [], [], [], [], [], []
    for l in range(DEPTH):
        i = l // 2
        h = rms_norm(x, W['g_mix_pre'][l])
        if l % 2 == 0:
            lam_init = 0.8 - 0.6 * math.exp(-0.3 * l)
            lam = (jnp.exp(jnp.sum(W['lambda_q1'][i].astype(jnp.float32) * W['lambda_k1'][i].astype(jnp.float32)))
                   - jnp.exp(jnp.sum(W['lambda_q2'][i].astype(jnp.float32) * W['lambda_k2'][i].astype(jnp.float32)))
                   + lam_init)
            q_lat, q_rope, dq, ckv, kr, dk, dv = attn_project(h, pos, W['w_attn_in'][i], W['g_q_norm'][i],
                                                              W['w_uq'][i], W['g_kv_norm'][i], W['w_uk'][i])
            o_lat, o_diff = attend(i, q_lat, q_rope, dq, ckv, kr, dk, dv, lam)
            o_mla = jnp.einsum('bthc,chv->bthv', o_lat.astype(h.dtype), W['w_uv'][i]).reshape(B, T, MLA_HEADS * MLA_V)
            o_dif = (rms_norm(o_diff.astype(h.dtype), W['g_diff_subln'][i]) * (1.0 - lam_init)).reshape(B, T, DIFF_HEADS * DIFF_V)
            mix = jnp.concatenate([o_mla, o_dif], axis=-1) @ W['w_attn_out'][i]
            lat.append(ckv)
            kro.append(kr)
            kd.append(dk)
            vd.append(dv)
        else:
            mix, S = gla_mixer(h, gla_state0[i], W['w_gla_in'][i], W['w_gla_gate'][i], W['b_gla_gate'][i],
                               W['g_gla_out'][i], W['w_gla_out'][i])
            gla.append(S)
        x = x + rms_norm(mix, W['g_mix_post'][l])
        h = rms_norm(x, W['g_ffn_pre'][l])
        y, c = conv_ffn(h, conv_state0[l], W['w_ffn_up'][l], W['w_ffn_conv'][l], W['b_ffn_conv'][l], W['w_ffn_down'][l])
        conv.append(c)
        x = x + rms_norm(y, W['g_ffn_post'][l])
    return x, jnp.stack(lat), jnp.stack(kro), jnp.stack(kd), jnp.stack(vd), jnp.stack(gla), jnp.stack(conv)


def setup_inputs(seed: int = 0) -> dict:
    key = jax.random.key(seed)
    ks = iter(jax.random.split(key, 64))

    def nrm(shape, scale):
        return jax.random.normal(next(ks), shape, jnp.float32) * scale

    def gain(shape):
        return 1.0 + nrm(shape, 0.02)

    n_pages = PAST_LEN // PAGE_SIZE
    n_pool = (DEC_BATCH * n_pages * POOL_NUM) // POOL_DEN
    LA, LG = N_ATTN_LAYERS, N_GLA_LAYERS
    d = {}
    d['x_prompt'] = nrm((BATCH, SEQ, D_MODEL), 1.0)
    d['x_sample'] = nrm((DEC_BATCH, DEC_SEQ, D_MODEL), 1.0)
    d['cache_mla_latent'] = nrm((LA, n_pool, PAGE_SIZE, KV_LORA), 1.0)
    d['cache_mla_rope'] = nrm((LA, n_pool, PAGE_SIZE, MLA_ROPE), 1.0)
    d['cache_diff_k'] = nrm((LA, n_pool, PAGE_SIZE, DIFF_HEADS, 2, DIFF_DH), 1.0)
    d['cache_diff_v'] = nrm((LA, n_pool, PAGE_SIZE, DIFF_HEADS, DIFF_V), 1.0)
    d['state_gla'] = nrm((LG, DEC_BATCH, GLA_HEADS, GLA_DK, GLA_DV), 1.0)
    d['state_ffn_conv'] = nrm((DEPTH, DEC_BATCH, CONV_W - 1, D_FF), 1.0)
    d['page_table'] = jax.random.permutation(next(ks), n_pool)[:DEC_BATCH * n_pages].reshape(DEC_BATCH, n_pages).astype(jnp.int32)
    d['g_mix_pre'] = gain((DEPTH, D_MODEL))
    d['g_mix_post'] = gain((DEPTH, D_MODEL))
    d['g_ffn_pre'] = gain((DEPTH, D_MODEL))
    d['g_ffn_post'] = gain((DEPTH, D_MODEL))
    d['w_attn_in'] = nrm((LA, D_MODEL, ATTN_IN), D_MODEL ** -0.5)
    d['g_q_norm'] = gain((LA, Q_LORA))
    d['w_uq'] = nrm((LA, Q_LORA, MLA_HEADS * (MLA_NOPE + MLA_ROPE)), Q_LORA ** -0.5)
    d['g_kv_norm'] = gain((LA, KV_LORA))
    d['w_uk'] = nrm((LA, KV_LORA, MLA_HEADS, MLA_NOPE), KV_LORA ** -0.5)
    d['w_uv'] = nrm((LA, KV_LORA, MLA_HEADS, MLA_V), KV_LORA ** -0.5)
    d['lambda_q1'] = nrm((LA, DIFF_DH), 0.1)
    d['lambda_k1'] = nrm((LA, DIFF_DH), 0.1)
    d['lambda_q2'] = nrm((LA, DIFF_DH), 0.1)
    d['lambda_k2'] = nrm((LA, DIFF_DH), 0.1)
    d['g_diff_subln'] = gain((LA, DIFF_V))
    d['w_attn_out'] = nrm((LA, ATTN_OUT, D_MODEL), ATTN_OUT ** -0.5)
    d['w_gla_in'] = nrm((LG, D_MODEL, GLA_IN), D_MODEL ** -0.5)
    d['w_gla_gate'] = nrm((LG, GLA_GATE_RANK, GLA_HEADS * GLA_DK), GLA_GATE_RANK ** -0.5)
    d['b_gla_gate'] = nrm((LG, GLA_HEADS * GLA_DK), 0.1)
    d['g_gla_out'] = gain((LG, GLA_DV))
    d['w_gla_out'] = nrm((LG, GLA_HEADS * GLA_DV, D_MODEL), (GLA_HEADS * GLA_DV) ** -0.5)
    d['w_ffn_up'] = nrm((DEPTH, D_MODEL, 2 * D_FF), D_MODEL ** -0.5)
    d['w_ffn_conv'] = nrm((DEPTH, CONV_W, D_FF), CONV_W ** -0.5)
    d['b_ffn_conv'] = nrm((DEPTH, D_FF), 0.02)
    d['w_ffn_down'] = nrm((DEPTH, D_FF, D_MODEL), D_FF ** -0.5)
    return d


def reference(x_prompt, x_sample, cache_mla_latent, cache_mla_rope, cache_diff_k, cache_diff_v, state_gla,
              state_ffn_conv, page_table, g_mix_pre, g_mix_post, g_ffn_pre, g_ffn_post, w_attn_in, g_q_norm,
              w_uq, g_kv_norm, w_uk, w_uv, lambda_q1, lambda_k1, lambda_q2, lambda_k2, g_diff_subln, w_attn_out,
              w_gla_in, w_gla_gate, b_gla_gate, g_gla_out, w_gla_out, w_ffn_up, w_ffn_conv, b_ffn_conv, w_ffn_down):
    W = dict(g_mix_pre=g_mix_pre, g_mix_post=g_mix_post, g_ffn_pre=g_ffn_pre, g_ffn_post=g_ffn_post,
             w_attn_in=w_attn_in, g_q_norm=g_q_norm, w_uq=w_uq, g_kv_norm=g_kv_norm, w_uk=w_uk, w_uv=w_uv,
             lambda_q1=lambda_q1, lambda_k1=lambda_k1, lambda_q2=lambda_q2, lambda_k2=lambda_k2,
             g_diff_subln=g_diff_subln, w_attn_out=w_attn_out, w_gla_in=w_gla_in, w_gla_gate=w_gla_gate,
             b_gla_gate=b_gla_gate, g_gla_out=g_gla_out, w_gla_out=w_gla_out, w_ffn_up=w_ffn_up,
             w_ffn_conv=w_ffn_conv, b_ffn_conv=b_ffn_conv, w_ffn_down=w_ffn_down)
    Bp, Sp, _ = x_prompt.shape
    gla0 = jnp.zeros((N_GLA_LAYERS, Bp, GLA_HEADS, GLA_DK, GLA_DV), jnp.float32)
    conv0 = jnp.zeros((DEPTH, Bp, CONV_W - 1, D_FF), x_prompt.dtype)

    def prompt_attend(i, q_lat, q_rope, dq, ckv, kr, dk, dv, lam):
        return prompt_attention(q_lat, q_rope, dq, ckv, kr, dk, dv, lam)

    def sample_attend(i, q_lat, q_rope, dq, ckv, kr, dk, dv, lam):
        return sample_attention(q_lat, q_rope, dq, ckv, kr, dk, dv, lam, cache_mla_latent[i], cache_mla_rope[i],
                                cache_diff_k[i], cache_diff_v[i], page_table)

    y_prompt, p_lat, p_rope, p_k, p_v, p_gla, p_conv = run_trunk(
        x_prompt, jnp.arange(Sp), prompt_attend, gla0, conv0, W)
    past = page_table.shape[1] * PAGE_SIZE
    y_sample, s_lat, s_rope, s_k, s_v, s_gla, s_conv = run_trunk(
        x_sample, past + jnp.arange(x_sample.shape[1]), sample_attend, state_gla, state_ffn_conv, W)
    return (y_prompt, y_sample, p_lat, p_rope, p_k, p_v, p_gla, p_conv, s_lat, s_rope, s_k, s_v, s_gla, s_conv)
```

```python
import functools
import math

import jax
import jax.numpy as jnp
from jax import lax
from jax.experimental import pallas as pl
from jax.experimental.pallas import tpu as pltpu

F32 = jnp.float32
BF16 = jnp.bfloat16

D_MODEL = 1024
DEPTH = 4
PAGE = 128
MLA_HEADS = 8
MLA_NOPE = 64
MLA_ROPE = 32
MLA_V = 64
Q_LORA = 256
KV_LORA = 128
DIFF_HEADS = 4
DIFF_DH = 64
DIFF_V = 128
GLA_HEADS = 4
GLA_DK = 128
GLA_DV = 256
GLA_RANK = 16
GLA_TAU = 16.0
D_FF = 2816
ROPE_THETA = 10000.0
EPS = 1e-6
NEG = -1e30
MLA_SCALE = (MLA_NOPE + MLA_ROPE) ** -0.5
DIFF_SCALE = DIFF_DH ** -0.5

LANES = 128
VMEM_LIMIT = 56 * 1024 * 1024

A_CQ, A_CKV, A_DQ, A_DK, A_DV, A_KR, A_W = 0, 256, 384, 896, 1408, 1920, 2048
G_Q, G_K, G_V, G_R, G_A, G_W = 0, 512, 1024, 2048, 3072, 3200

NT = (((1,), (1,)), ((), ()))
TN = (((0,), (0,)), ((), ()))


def _cp(*sem):
    return pltpu.CompilerParams(dimension_semantics=sem, vmem_limit_bytes=VMEM_LIMIT)


def _rms(x, g):
    return x * lax.rsqrt(jnp.mean(x * x, axis=-1, keepdims=True) + EPS) * g


def _dot(a, b):
    return jnp.dot(a, b, preferred_element_type=F32)


def _const_spec(shape):
    nd = len(shape)
    return pl.BlockSpec(shape, lambda *_: (0,) * nd)


def _rope_tables(pos):
    pos = pos.astype(F32)[:, None]
    lane = jnp.arange(LANES)

    def one(d):
        half = d // 2
        inv = ROPE_THETA ** (-jnp.arange(half, dtype=F32) * (2.0 / d))
        ang = pos * inv
        cos = jnp.tile(jnp.cos(ang), (1, LANES // half))
        sin = jnp.tile(jnp.sin(ang), (1, LANES // half))
        first = (lane % d) < half
        return [cos, jnp.where(first, -sin, 0.0), jnp.where(first, 0.0, sin)]

    return jnp.concatenate(one(DIFF_DH) + one(MLA_ROPE), axis=1)


def _rope(x, cos, sin_a, sin_b, half):
    return x * cos + pltpu.roll(x, LANES - half, 1) * sin_a + pltpu.roll(x, half, 1) * sin_b


def _attn_in_kernel(x_ref, g_ref, win_ref, gq_ref, wuq_ref, gkv_ref, wuk_ref, tab_ref,
                    qm_ref, kcat_ref, dq_ref, dkb_ref, dvb_ref, ckv_ref, kr_ref, dk_ref, dv_ref):
    h = _rms(x_ref[...], g_ref[...]).astype(BF16)
    p = _dot(h, win_ref[...])
    tab = tab_ref[...]
    c64, a64, b64, c32, a32, b32 = [tab[:, i * LANES:(i + 1) * LANES] for i in range(6)]

    cqn = _rms(p[:, A_CQ:A_CKV], gq_ref[...]).astype(BF16)
    q = _dot(cqn, wuq_ref[...])
    nope_w = MLA_HEADS * MLA_NOPE
    qlat = _dot(q[:, :nope_w].astype(BF16), wuk_ref[...])
    for hh in range(MLA_HEADS):
        ql = qlat[:, hh * LANES:(hh + 1) * LANES]
        qr = _rope(q[:, nope_w + hh * LANES:nope_w + (hh + 1) * LANES], c32, a32, b32, MLA_ROPE // 2)
        qm_ref[hh] = (jnp.concatenate([ql, qr], axis=1) * MLA_SCALE).astype(qm_ref.dtype)

    ckv = _rms(p[:, A_CKV:A_DQ], gkv_ref[...])
    ckv_ref[...] = ckv
    krs = _rope(p[:, A_KR:A_W], c32, a32, b32, MLA_ROPE // 2)
    kr_ref[...] = krs[:, :MLA_ROPE]
    kcat_ref[...] = jnp.concatenate([ckv, krs], axis=1).astype(kcat_ref.dtype)
    for s in range(4):
        sl = slice(s * LANES, (s + 1) * LANES)
        dq = _rope(p[:, A_DQ + s * LANES:A_DQ + (s + 1) * LANES], c64, a64, b64, DIFF_DH // 2)
        dq_ref[:, sl] = (dq * DIFF_SCALE).astype(dq_ref.dtype)
        dk = _rope(p[:, A_DK + s * LANES:A_DK + (s + 1) * LANES], c64, a64, b64, DIFF_DH // 2)
        dk_ref[:, sl] = dk
        dkb_ref[:, sl] = dk.astype(dkb_ref.dtype)
    dv = p[:, A_DV:A_KR]
    dv_ref[...] = dv
    dvb_ref[...] = dv.astype(dvb_ref.dtype)


def _attn_in(x, g, w, tab, act_dtype, tm):
    nb, t, d = x.shape
    grid = (nb, t // tm)
    row = lambda c: pl.BlockSpec((None, tm, c), lambda b, i: (b, i, 0))
    out_shape = (
        jax.ShapeDtypeStruct((nb, MLA_HEADS, t, 2 * LANES), act_dtype),
        jax.ShapeDtypeStruct((nb, t, 2 * LANES), act_dtype),
        jax.ShapeDtypeStruct((nb, t, 512), act_dtype),
        jax.ShapeDtypeStruct((nb, t, 512), act_dtype),
        jax.ShapeDtypeStruct((nb, t, 512), act_dtype),
        jax.ShapeDtypeStruct((nb, t, KV_LORA), F32),
        jax.ShapeDtypeStruct((nb, t, MLA_ROPE), F32),
        jax.ShapeDtypeStruct((nb, t, 512), F32),
        jax.ShapeDtypeStruct((nb, t, 512), F32),
    )
    out_specs = (
        pl.BlockSpec((None, MLA_HEADS, tm, 2 * LANES), lambda b, i: (b, 0, i, 0)),
        row(2 * LANES), row(512), row(512), row(512), row(KV_LORA), row(MLA_ROPE), row(512), row(512),
    )
    in_specs = [
        row(d), _const_spec((1, d)), _const_spec(w['w_in'].shape), _const_spec((1, Q_LORA)),
        _const_spec(w['w_uq'].shape), _const_spec((1, KV_LORA)), _const_spec(w['w_ukbd'].shape),
        pl.BlockSpec((tm, 6 * LANES), lambda b, i: (i, 0)),
    ]
    return pl.pallas_call(
        _attn_in_kernel, grid=grid, in_specs=in_specs, out_specs=out_specs, out_shape=out_shape,
        compiler_params=_cp("parallel", "parallel"), name="attn_in",
    )(x, g, w['w_in'], w['g_q'], w['w_uq'], w['g_kv'], w['w_ukbd'], tab)


def _online_update(s, v, m_ref, l_ref, acc_ref):
    m_prev = m_ref[...]
    m_new = jnp.maximum(m_prev, jnp.max(s, axis=1, keepdims=True))
    alpha = jnp.exp(m_prev - m_new)
    p = jnp.exp(s - m_new)
    l_ref[...] = alpha * l_ref[...] + jnp.sum(p, axis=1, keepdims=True)
    acc_ref[...] = alpha * acc_ref[...] + _dot(p.astype(v.dtype), v)
    m_ref[...] = m_new


def _lambda(lam_ref, lam_init):
    lv = lam_ref[...]
    s1 = jnp.sum(lv[0:1] * lv[1:2], axis=1, keepdims=True)
    s2 = jnp.sum(lv[2:3] * lv[3:4], axis=1, keepdims=True)
    return jnp.exp(s1) - jnp.exp(s2) + lam_init


def _split_maps(dqh):
    lane = lax.broadcasted_iota(jnp.int32, dqh.shape, 1)
    zero = jnp.zeros_like(dqh)
    return jnp.concatenate([jnp.where(lane < DIFF_DH, dqh, zero), jnp.where(lane >= DIFF_DH, dqh, zero)], axis=0)


def _flash_kernel(qm_ref, kcat_ref, dq_ref, dk_ref, dv_ref, lam_ref, olat_ref, odiff_ref,
                  m1, l1, acc1, m2, l2, acc2, *, tq, tk, lam_init):
    qi = pl.program_id(1)
    ki = pl.program_id(2)
    last = ((qi + 1) * tq - 1) // tk

    @pl.when(ki == 0)
    def _():
        m1[...] = jnp.full(m1.shape, NEG, F32)
        l1[...] = jnp.zeros(l1.shape, F32)
        acc1[...] = jnp.zeros(acc1.shape, F32)
        m2[...] = jnp.full(m2.shape, NEG, F32)
        l2[...] = jnp.zeros(l2.shape, F32)
        acc2[...] = jnp.zeros(acc2.shape, F32)

    def step(masked):
        def mask(s):
            if not masked:
                return s
            row = lax.broadcasted_iota(jnp.int32, s.shape, 0) & (tq - 1)
            col = lax.broadcasted_iota(jnp.int32, s.shape, 1)
            return jnp.where(ki * tk + col <= qi * tq + row, s, NEG)

        q = qm_ref[...].reshape(MLA_HEADS * tq, 2 * LANES)
        k = kcat_ref[...]
        s = mask(lax.dot_general(q, k, NT, preferred_element_type=F32))
        _online_update(s, k[:, :KV_LORA], m1, l1, acc1)
        for h in range(DIFF_HEADS):
            sl = slice(h * LANES, (h + 1) * LANES)
            qq = _split_maps(dq_ref[:, sl])
            s = mask(lax.dot_general(qq, dk_ref[:, sl], NT, preferred_element_type=F32))
            _online_update(s, dv_ref[:, sl], m2.at[h], l2.at[h], acc2.at[h])

    @pl.when((ki + 1) * tk - 1 <= qi * tq)
    def _():
        step(False)

    @pl.when(jnp.logical_and((ki + 1) * tk - 1 > qi * tq, ki <= last))
    def _():
        step(True)

    @pl.when(ki == last)
    def _():
        inv = 1.0 / l1[...]
        for h in range(MLA_HEADS):
            rows = slice(h * tq, (h + 1) * tq)
            olat_ref[:, h * LANES:(h + 1) * LANES] = (acc1[rows, :] * inv[rows, :]).astype(olat_ref.dtype)
        lam = _lambda(lam_ref, lam_init)
        for h in range(DIFF_HEADS):
            o = acc2[h] / l2[h]
            odiff_ref[:, h * LANES:(h + 1) * LANES] = o[:tq] - lam * o[tq:]


def _flash(qm, kcat, dq, dk, dv, lam, lam_init, tq, tk):
    nb, _, s, _ = qm.shape
    nq, nk = s // tq, s // tk

    def kidx(b, qi, ki):
        return (b, jnp.minimum(ki, ((qi + 1) * tq - 1) // tk), 0)

    in_specs = [
        pl.BlockSpec((None, MLA_HEADS, tq, 2 * LANES), lambda b, qi, ki: (b, 0, qi, 0)),
        pl.BlockSpec((None, tk, 2 * LANES), kidx),
        pl.BlockSpec((None, tq, 512), lambda b, qi, ki: (b, qi, 0)),
        pl.BlockSpec((None, tk, 512), kidx),
        pl.BlockSpec((None, tk, 512), kidx),
        _const_spec((8, LANES)),
    ]
    out_specs = (
        pl.BlockSpec((None, tq, MLA_HEADS * LANES), lambda b, qi, ki: (b, qi, 0)),
        pl.BlockSpec((None, tq, 512), lambda b, qi, ki: (b, qi, 0)),
    )
    out_shape = (jax.ShapeDtypeStruct((nb, s, MLA_HEADS * LANES), BF16),
                 jax.ShapeDtypeStruct((nb, s, 512), F32))
    scratch = [
        pltpu.VMEM((MLA_HEADS * tq, 1), F32), pltpu.VMEM((MLA_HEADS * tq, 1), F32),
        pltpu.VMEM((MLA_HEADS * tq, KV_LORA), F32),
        pltpu.VMEM((DIFF_HEADS, 2 * tq, 1), F32), pltpu.VMEM((DIFF_HEADS, 2 * tq, 1), F32),
        pltpu.VMEM((DIFF_HEADS, 2 * tq, DIFF_V), F32),
    ]
    return pl.pallas_call(
        functools.partial(_flash_kernel, tq=tq, tk=tk, lam_init=lam_init),
        grid=(nb, nq, nk), in_specs=in_specs, out_specs=out_specs, out_shape=out_shape,
        scratch_shapes=scratch, compiler_params=_cp("parallel", "parallel", "arbitrary"), name="flash_prompt",
    )(qm, kcat, dq, dk, dv, lam)


def _decode_kernel(pt_ref, qm_ref, dq_ref, kcat_ref, dkn_ref, dvn_ref, lam_ref, *rest, pps, t_new, lam_init):
    lat_refs = rest[0 * pps:1 * pps]
    rope_refs = rest[1 * pps:2 * pps]
    dk_refs = rest[2 * pps:3 * pps]
    dv_refs = rest[3 * pps:4 * pps]
    olat_ref, odiff_ref, qs, m1, l1, acc1, m2, l2, acc2 = rest[4 * pps:]
    c = pl.program_id(1)
    nc = pl.num_programs(1)
    rows = MLA_HEADS * t_new

    @pl.when(c == 0)
    def _():
        qs[...] = qm_ref[...].reshape(rows, 2 * LANES)
        m1[...] = jnp.full(m1.shape, NEG, F32)
        l1[...] = jnp.zeros(l1.shape, F32)
        acc1[...] = jnp.zeros(acc1.shape, F32)
        m2[...] = jnp.full(m2.shape, NEG, F32)
        l2[...] = jnp.zeros(l2.shape, F32)
        acc2[...] = jnp.zeros(acc2.shape, F32)

    q = qs[...]
    ql = q[:, :KV_LORA].astype(BF16)
    qr = q[:, KV_LORA:KV_LORA + MLA_ROPE].astype(BF16)
    dq = dq_ref[...]

    lat = [r[...].astype(BF16) for r in lat_refs]
    s = jnp.concatenate(
        [lax.dot_general(ql, lat[j], NT, preferred_element_type=F32) + _dot(qr, rope_refs[j][...].astype(BF16))
         for j in range(pps)], axis=1)
    m_prev = m1[...]
    m_new = jnp.maximum(m_prev, jnp.max(s, axis=1, keepdims=True))
    alpha = jnp.exp(m_prev - m_new)
    p = jnp.exp(s - m_new)
    l1[...] = alpha * l1[...] + jnp.sum(p, axis=1, keepdims=True)
    pb = p.astype(BF16)
    pv = _dot(pb[:, :PAGE], lat[0])
    for j in range(1, pps):
        pv = pv + _dot(pb[:, j * PAGE:(j + 1) * PAGE], lat[j])
    acc1[...] = alpha * acc1[...] + pv
    m1[...] = m_new

    for h in range(DIFF_HEADS):
        q1 = dq[:, h * LANES:h * LANES + DIFF_DH].astype(BF16)
        q2 = dq[:, h * LANES + DIFF_DH:(h + 1) * LANES].astype(BF16)
        s = jnp.concatenate(
            [jnp.concatenate([_dot(q1, dk_refs[j][h, 0].astype(BF16)), _dot(q2, dk_refs[j][h, 1].astype(BF16))], axis=0)
             for j in range(pps)], axis=1)
        m_prev = m2[h]
        m_new = jnp.maximum(m_prev, jnp.max(s, axis=1, keepdims=True))
        alpha = jnp.exp(m_prev - m_new)
        p = jnp.exp(s - m_new)
        l2[h] = alpha * l2[h] + jnp.sum(p, axis=1, keepdims=True)
        pb = p.astype(BF16)
        pv = None
        for j in range(pps):
            vh = dv_refs[j][pl.ds(h, PAGE, stride=DIFF_HEADS), :].astype(BF16)
            t = _dot(pb[:, j * PAGE:(j + 1) * PAGE], vh)
            pv = t if pv is None else pv + t
        acc2[h] = alpha * acc2[h] + pv
        m2[h] = m_new

    @pl.when(c == nc - 1)
    def _():
        kc = kcat_ref[...]
        s = lax.dot_general(q, kc, NT, preferred_element_type=F32)
        row = lax.broadcasted_iota(jnp.int32, s.shape, 0) & (t_new - 1)
        col = lax.broadcasted_iota(jnp.int32, s.shape, 1)
        s = jnp.where(col <= row, s, NEG)
        _online_update(s, kc[:, :KV_LORA], m1, l1, acc1)
        o = acc1[...] / l1[...]
        for h in range(MLA_HEADS):
            olat_ref[:, h * LANES:(h + 1) * LANES] = o[h * t_new:(h + 1) * t_new].astype(olat_ref.dtype)
        lam = _lambda(lam_ref, lam_init)
        dkn = dkn_ref[...]
        dvn = dvn_ref[...]
        for h in range(DIFF_HEADS):
            sl = slice(h * LANES, (h + 1) * LANES)
            qq = _split_maps(dq[:, sl])
            s = lax.dot_general(qq, dkn[:, sl], NT, preferred_element_type=F32)
            row = lax.broadcasted_iota(jnp.int32, s.shape, 0) & (t_new - 1)
            col = lax.broadcasted_iota(jnp.int32, s.shape, 1)
            s = jnp.where(col <= row, s, NEG)
            _online_update(s, dvn[:, sl], m2.at[h], l2.at[h], acc2.at[h])
            o = acc2[h] / l2[h]
            odiff_ref[:, sl] = o[:t_new] - lam * o[t_new:]


def _decode(layer, qm, dq, kcat, dkn, dvn, lam, lam_init, lat_pool, ropeT_pool, dkT_pool, dv_pool, page_table, pps):
    _, _, tot, _ = qm.shape
    nbatch, n_pages = page_table.shape
    t_new = tot // nbatch
    nc = n_pages // pps
    rows = MLA_HEADS * t_new

    def page(j):
        return lambda b, c, pt: (layer, pt[b, c * pps + j]) + (0,) * 2

    def page4(j):
        return lambda b, c, pt: (layer, pt[b, c * pps + j], 0, 0, 0, 0)

    new = lambda w: pl.BlockSpec((None, t_new, w), lambda b, c, pt: (0, b, 0))
    in_specs = [
        pl.BlockSpec((None, MLA_HEADS, t_new, 2 * LANES), lambda b, c, pt: (0, 0, b, 0)),
        new(512), new(2 * LANES), new(512), new(512),
        pl.BlockSpec((8, LANES), lambda b, c, pt: (0, 0)),
    ]
    in_specs += [pl.BlockSpec((None, None, PAGE, KV_LORA), page(j)) for j in range(pps)]
    in_specs += [pl.BlockSpec((None, None, MLA_ROPE, PAGE), page(j)) for j in range(pps)]
    in_specs += [pl.BlockSpec((None, None, DIFF_HEADS, 2, DIFF_DH, PAGE), page4(j)) for j in range(pps)]
    in_specs += [pl.BlockSpec((None, None, PAGE * DIFF_HEADS, DIFF_V), page(j)) for j in range(pps)]
    out_specs = (new(MLA_HEADS * LANES), new(512))
    out_shape = (jax.ShapeDtypeStruct((1, tot, MLA_HEADS * LANES), F32),
                 jax.ShapeDtypeStruct((1, tot, 512), F32))
    scratch = [
        pltpu.VMEM((rows, 2 * LANES), F32),
        pltpu.VMEM((rows, 1), F32), pltpu.VMEM((rows, 1), F32), pltpu.VMEM((rows, KV_LORA), F32),
        pltpu.VMEM((DIFF_HEADS, 2 * t_new, 1), F32), pltpu.VMEM((DIFF_HEADS, 2 * t_new, 1), F32),
        pltpu.VMEM((DIFF_HEADS, 2 * t_new, DIFF_V), F32),
    ]
    grid_spec = pltpu.PrefetchScalarGridSpec(
        num_scalar_prefetch=1, grid=(nbatch, nc), in_specs=in_specs, out_specs=out_specs, scratch_shapes=scratch)
    return pl.pallas_call(
        functools.partial(_decode_kernel, pps=pps, t_new=t_new, lam_init=lam_init),
        grid_spec=grid_spec, out_shape=out_shape,
        compiler_params=_cp("parallel", "arbitrary"), name="decode_attn",
    )(page_table, qm, dq, kcat, dkn, dvn, lam,
      *([lat_pool] * pps), *([ropeT_pool] * pps), *([dkT_pool] * pps), *([dv_pool] * pps))


def _attn_out_kernel(x_ref, olat_ref, odiff_ref, wuv_ref, gsub_ref, wo_ref, gpost_ref, o_ref, *, scale):
    o_mla = _dot(olat_ref[...].astype(BF16), wuv_ref[...])
    od = odiff_ref[...]
    parts = [o_mla]
    for h in range(DIFF_HEADS):
        parts.append(_rms(od[:, h * LANES:(h + 1) * LANES], gsub_ref[...]) * scale)
    mix = _dot(jnp.concatenate(parts, axis=1).astype(BF16), wo_ref[...])
    o_ref[...] = x_ref[...] + _rms(mix, gpost_ref[...])


def _attn_out(x, olat, odiff, w, gpost, scale, tm):
    nb, t, d = x.shape
    row = lambda c: pl.BlockSpec((None, tm, c), lambda b, i: (b, i, 0))
    in_specs = [row(d), row(MLA_HEADS * LANES), row(512), _const_spec(w['w_uvbd'].shape), _const_spec((1, DIFF_V)),
                _const_spec(w['w_out'].shape), _const_spec((1, d))]
    return pl.pallas_call(
        functools.partial(_attn_out_kernel, scale=scale), grid=(nb, t // tm), in_specs=in_specs,
        out_specs=row(d), out_shape=jax.ShapeDtypeStruct(x.shape, F32),
        compiler_params=_cp("parallel", "parallel"), name="attn_out",
    )(x, olat, odiff, w['w_uvbd'], w['g_sub'], w['w_out'], gpost)


FFN_CHUNK = 256


def _ffn_kernel(*refs, period, has_prev):
    if has_prev:
        x_ref, gpre_ref, gpost_ref, wup_ref, wconv_ref, bconv_ref, wdown_ref, prev_ref, o_ref, st_ref, acc, carry = refs
    else:
        x_ref, gpre_ref, gpost_ref, wup_ref, wconv_ref, bconv_ref, wdown_ref, o_ref, st_ref, acc, carry = refs
    tm = x_ref.shape[0]
    i = pl.program_id(1)
    x = x_ref[...]
    h = _rms(x, gpre_ref[...]).astype(BF16)
    row = lax.broadcasted_iota(jnp.int32, (tm, FFN_CHUNK), 0)
    t = row & (period - 1) if period < tm else row

    if not has_prev:
        @pl.when(i == 0)
        def _():
            carry[...] = jnp.zeros(carry.shape, F32)

    acc[...] = jnp.zeros(acc.shape, F32)
    for c in range(D_FF // FFN_CHUNK):
        sl = slice(c * FFN_CHUNK, (c + 1) * FFN_CHUNK)
        a = _dot(h, wup_ref[:, sl])
        v = _dot(h, wup_ref[:, D_FF + c * FFN_CHUNK:D_FF + (c + 1) * FFN_CHUNK])
        if has_prev:
            nseq = tm // period
            pr = prev_ref[:, :, sl]
            p0 = jnp.broadcast_to(pr[:, 0:1, :], (nseq, period, FFN_CHUNK)).reshape(tm, FFN_CHUNK)
            p1 = jnp.broadcast_to(pr[:, 1:2, :], (nseq, period, FFN_CHUNK)).reshape(tm, FFN_CHUNK)
            st_ref[:, :, sl] = a.reshape(nseq, period, FFN_CHUNK)[:, period - 2:, :]
        else:
            cr = carry[:, sl]
            p0 = cr[6:7, :]
            p1 = cr[7:8, :]
            carry[:, sl] = a[tm - 8:, :]
            st_ref[:, sl] = a[tm - 2:, :]
        am1 = jnp.where(t == 0, p1, pltpu.roll(a, 1, 0))
        am2 = jnp.where(t == 0, p0, jnp.where(t == 1, p1, pltpu.roll(a, 2, 0)))
        wc = wconv_ref[:, sl]
        ac = bconv_ref[:, sl] + wc[0:1] * am2 + wc[1:2] * am1 + wc[2:3] * a
        gelu = 0.5 * ac * (1.0 + lax.erf(ac * math.sqrt(0.5)))
        z = (gelu * v).astype(BF16)
        acc[...] += _dot(z, wdown_ref[sl, :])
    o_ref[...] = x + _rms(acc[...], gpost_ref[...])


def _ffn(x, gpre, gpost, w_up, w_conv, b_conv, w_down, prev, tm):
    nb, t, d = x.shape
    has_prev = prev is not None
    row = pl.BlockSpec((None, tm, d), lambda b, i: (b, i, 0))
    in_specs = [row, _const_spec((1, d)), _const_spec((1, d)), _const_spec(w_up.shape), _const_spec((3, D_FF)),
                _const_spec((1, D_FF)), _const_spec(w_down.shape)]
    args = [x, gpre, gpost, w_up, w_conv, b_conv, w_down]
    if has_prev:
        nseq = prev.shape[0]
        period = t // nseq
        spt = tm // period
        in_specs.append(pl.BlockSpec((spt, 2, D_FF), lambda b, i: (i, 0, 0)))
        args.append(prev)
        st_spec = pl.BlockSpec((spt, 2, D_FF), lambda b, i: (i, 0, 0))
        st_shape = jax.ShapeDtypeStruct((nseq, 2, D_FF), F32)
    else:
        period = t
        st_spec = pl.BlockSpec((None, 2, D_FF), lambda b, i: (b, 0, 0))
        st_shape = jax.ShapeDtypeStruct((nb, 2, D_FF), F32)
    return pl.pallas_call(
        functools.partial(_ffn_kernel, period=period, has_prev=has_prev), grid=(nb, t // tm), in_specs=in_specs,
        out_specs=(row, st_spec), out_shape=(jax.ShapeDtypeStruct(x.shape, F32), st_shape),
        scratch_shapes=[pltpu.VMEM((tm, d), F32), pltpu.VMEM((8, D_FF), F32)],
        compiler_params=_cp("parallel", "arbitrary"), name="conv_ffn",
    )(*args)


def _gla_in_kernel(x_ref, g_ref, win_ref, wg_ref, bg_ref, q_ref, k_ref, v_ref, r_ref, ga_ref):
    h = _rms(x_ref[...], g_ref[...]).astype(BF16)
    p = _dot(h, win_ref[...])
    q_ref[...] = p[:, G_Q:G_K] * (GLA_DK ** -0.5)
    k_ref[...] = p[:, G_K:G_V]
    v_ref[...] = p[:, G_V:G_R].astype(v_ref.dtype)
    r_ref[...] = p[:, G_R:G_A]
    z = _dot(p[:, G_A:G_W].astype(BF16), wg_ref[...]) + bg_ref[...]
    ga_ref[...] = (jnp.minimum(z, 0.0) - jnp.log1p(jnp.exp(-jnp.abs(z)))) * (1.0 / GLA_TAU)


def _gla_in(x, g, w, tm):
    nb, t, d = x.shape
    row = lambda c: pl.BlockSpec((None, tm, c), lambda b, i: (b, i, 0))
    sds = lambda c, dt: jax.ShapeDtypeStruct((nb, t, c), dt)
    return pl.pallas_call(
        _gla_in_kernel, grid=(nb, t // tm),
        in_specs=[row(d), _const_spec((1, d)), _const_spec(w['w_in'].shape), _const_spec(w['w_gate'].shape),
                  _const_spec((1, 512))],
        out_specs=(row(512), row(512), row(1024), row(1024), row(512)),
        out_shape=(sds(512, F32), sds(512, F32), sds(1024, BF16), sds(1024, F32), sds(512, F32)),
        compiler_params=_cp("parallel", "parallel"), name="gla_in",
    )(x, g, w['w_in'], w['w_gate'], w['b_gate'])


def _gla_chunk(q_ref, k_ref, v_ref, g_ref, b_ref, base, c, state, sub):
    mm = BF16 if c >= 16 else F32
    q = q_ref[base:base + c, :]
    k = k_ref[base:base + c, :]
    v = v_ref[base:base + c, :].astype(mm)
    row = lax.broadcasted_iota(jnp.int32, (c, GLA_DK), 0)
    b = g_ref[base:base + c, :]
    step = 1
    while step < c:
        b = b + jnp.where(row >= step, pltpu.roll(b, step, 0), 0.0)
        step *= 2
    b_ref[...] = b
    b_end = b_ref[c - 1:c, :]
    lane_c = lax.broadcasted_iota(jnp.int32, (sub, c), 1)
    rsub = lax.broadcasted_iota(jnp.int32, (sub, c), 0)
    blocks = []
    for i in range(c // sub):
        r0 = i * sub
        qi = q[r0:r0 + sub]
        bi = b[r0:r0 + sub]
        if i == 0:
            a_i = jnp.zeros((sub, c), F32)
        else:
            beta = b_ref[r0 - 1:r0, :]
            qt = (qi * jnp.exp(bi - beta)).astype(mm)
            kt = (k * jnp.exp(jnp.minimum(beta - b, 0.0))).astype(mm)
            a_i = lax.dot_general(qt, kt, NT, preferred_element_type=F32)
            a_i = jnp.where(lane_c < r0, a_i, 0.0)
        for s in range(sub):
            r = r0 + s
            ks = k_ref[base + r:base + r + 1, :]
            bs = b_ref[r:r + 1, :]
            w = qi * ks * jnp.exp(jnp.minimum(bi - bs, 0.0))
            col = jnp.sum(w, axis=1, keepdims=True)
            a_i = jnp.where(jnp.logical_and(lane_c == r, rsub >= s), col, a_i)
        blocks.append(a_i)
    a = blocks[0] if len(blocks) == 1 else jnp.concatenate(blocks, axis=0)
    qd = (q * jnp.exp(b)).astype(mm)
    o = _dot(qd, state.astype(mm)) + _dot(a.astype(mm), v)
    kd = (k * jnp.exp(b_end - b)).astype(mm)
    rr = lax.broadcasted_iota(jnp.int32, (GLA_DK, GLA_DK), 0)
    cc = lax.broadcasted_iota(jnp.int32, (GLA_DK, GLA_DK), 1)
    decay = jnp.sum(jnp.where(rr == cc, jnp.exp(b_end), 0.0), axis=1, keepdims=True)
    new_state = decay * state + lax.dot_general(kd, v, TN, preferred_element_type=F32)
    return o, new_state


def _gla_kernel(*refs, chunk, sub, has_s0):
    if has_s0:
        q_ref, k_ref, v_ref, g_ref, s0_ref, o_ref, s_ref, st, b_scr = refs
    else:
        q_ref, k_ref, v_ref, g_ref, o_ref, s_ref, st, b_scr = refs
    i = pl.program_id(2)

    @pl.when(i == 0)
    def _():
        st[...] = s0_ref[...] if has_s0 else jnp.zeros(st.shape, F32)

    state = st[...]
    tt = q_ref.shape[0]
    for n in range(tt // chunk):
        o, state = _gla_chunk(q_ref, k_ref, v_ref, g_ref, b_scr, n * chunk, chunk, state, sub)
        o_ref[n * chunk:(n + 1) * chunk, :] = o
    st[...] = state
    s_ref[...] = state


def _gla(q, k, v, g, s0, layer, chunk, sub, tt):
    nb, t, _ = q.shape
    has_s0 = s0 is not None
    hs = lambda w: pl.BlockSpec((None, tt, w), lambda b, h, i: (b, i, h))
    in_specs = [hs(GLA_DK), hs(GLA_DK), hs(GLA_DV), hs(GLA_DK)]
    args = [q, k, v, g]
    if has_s0:
        in_specs.append(pl.BlockSpec((None, None, None, GLA_DK, GLA_DV), lambda b, h, i: (layer, b, h, 0, 0)))
        args.append(s0)
    return pl.pallas_call(
        functools.partial(_gla_kernel, chunk=chunk, sub=sub, has_s0=has_s0),
        grid=(nb, GLA_HEADS, t // tt), in_specs=in_specs,
        out_specs=(hs(GLA_DV), pl.BlockSpec((None, None, GLA_DK, GLA_DV), lambda b, h, i: (b, h, 0, 0))),
        out_shape=(jax.ShapeDtypeStruct((nb, t, GLA_HEADS * GLA_DV), F32),
                   jax.ShapeDtypeStruct((nb, GLA_HEADS, GLA_DK, GLA_DV), F32)),
        scratch_shapes=[pltpu.VMEM((GLA_DK, GLA_DV), F32), pltpu.VMEM((chunk, GLA_DK), F32)],
        compiler_params=_cp("parallel", "parallel", "arbitrary"), name="gla_core",
    )(*args)


def _gla_out_kernel(x_ref, o_ref, r_ref, gout_ref, wo_ref, gpost_ref, out_ref):
    o = o_ref[...]
    parts = [_rms(o[:, h * GLA_DV:(h + 1) * GLA_DV], gout_ref[...]) for h in range(GLA_HEADS)]
    y = jnp.concatenate(parts, axis=1) * jax.nn.silu(r_ref[...])
    mix = _dot(y.astype(BF16), wo_ref[...])
    out_ref[...] = x_ref[...] + _rms(mix, gpost_ref[...])


def _gla_out(x, o, r, gout, wo, gpost, tm):
    nb, t, d = x.shape
    row = lambda c: pl.BlockSpec((None, tm, c), lambda b, i: (b, i, 0))
    return pl.pallas_call(
        _gla_out_kernel, grid=(nb, t // tm),
        in_specs=[row(d), row(1024), row(1024), _const_spec((1, GLA_DV)), _const_spec(wo.shape), _const_spec((1, d))],
        out_specs=row(d), out_shape=jax.ShapeDtypeStruct(x.shape, F32),
        compiler_params=_cp("parallel", "parallel"), name="gla_out",
    )(x, o, r, gout, wo, gpost)


def _prep_attn(i, W):
    w_in = W['w_attn_in'][i]
    cq, ckv, kr, dq, dk, dv = jnp.split(w_in, [256, 384, 416, 928, 1440], axis=1)
    w_in2 = jnp.concatenate([cq, ckv, dq, dk, dv, kr, jnp.zeros((D_MODEL, A_W - A_KR - MLA_ROPE), F32)], axis=1)
    wq = W['w_uq'][i].reshape(Q_LORA, MLA_HEADS, MLA_NOPE + MLA_ROPE)
    w_nope = wq[:, :, :MLA_NOPE].reshape(Q_LORA, MLA_HEADS * MLA_NOPE)
    w_rope = jnp.pad(wq[:, :, MLA_NOPE:], ((0, 0), (0, 0), (0, LANES - MLA_ROPE))).reshape(Q_LORA, MLA_HEADS * LANES)
    eye = jnp.eye(MLA_HEADS, dtype=F32)
    w_ukbd = jnp.einsum('chn,hg->hngc', W['w_uk'][i], eye).reshape(MLA_HEADS * MLA_NOPE, MLA_HEADS * KV_LORA)
    w_uvbd = jnp.einsum('chv,hg->hcgv', W['w_uv'][i], eye).reshape(MLA_HEADS * KV_LORA, MLA_HEADS * MLA_V)
    lam = jnp.zeros((8, LANES), F32)
    for r, name in enumerate(('lambda_q1', 'lambda_k1', 'lambda_q2', 'lambda_k2')):
        lam = lam.at[r, :DIFF_DH].set(W[name][i])
    return dict(
        w_in=w_in2.astype(BF16), g_q=W['g_q_norm'][i][None], w_uq=jnp.concatenate([w_nope, w_rope], 1).astype(BF16),
        g_kv=W['g_kv_norm'][i][None], w_ukbd=w_ukbd.astype(BF16), w_uvbd=w_uvbd.astype(BF16),
        g_sub=W['g_diff_subln'][i][None], w_out=W['w_attn_out'][i].astype(BF16), lam=lam)


def _prep_gla(i, W):
    w_in = W['w_gla_in'][i]
    q, k, v, a, r = jnp.split(w_in, [512, 1024, 2048, 2064], axis=1)
    w_in2 = jnp.concatenate([q, k, v, r, a, jnp.zeros((D_MODEL, G_W - G_A - GLA_RANK), F32)], axis=1)
    w_gate = jnp.pad(W['w_gla_gate'][i], ((0, LANES - GLA_RANK), (0, 0)))
    return dict(w_in=w_in2.astype(BF16), w_gate=w_gate.astype(BF16), b_gate=W['b_gla_gate'][i][None],
                g_out=W['g_gla_out'][i][None], w_out=W['w_gla_out'][i].astype(BF16))


def _tile(t, pref):
    return pref if t % pref == 0 else t


def _trunk(x, tab, W, prep, sample):
    nb, t, d = x.shape
    tm = _tile(t, 512)
    act = F32 if sample is not None else BF16
    lat, kro, kd, vd, gla, conv = [], [], [], [], [], []
    for l in range(DEPTH):
        i = l // 2
        g_pre = W['g_mix_pre'][l][None]
        g_post = W['g_mix_post'][l][None]
        if l % 2 == 0:
            w = prep[l]
            lam_init = 0.8 - 0.6 * math.exp(-0.3 * l)
            qm, kcat, dq, dkb, dvb, ckv, kr, dk, dv = _attn_in(x, g_pre, w, tab, act, tm)
            if sample is None:
                olat, odiff = _flash(qm, kcat, dq, dkb, dvb, w['lam'], lam_init, _tile(t, 128), _tile(t, 512))
            else:
                olat, odiff = _decode(i, qm, dq, kcat, dkb, dvb, w['lam'], lam_init, sample['lat'], sample['ropeT'],
                                      sample['dkT'], sample['dv'], sample['page_table'], sample['pps'])
            x = _attn_out(x, olat, odiff, w, g_post, 1.0 - lam_init, tm)
            lat.append(ckv)
            kro.append(kr)
            kd.append(dk)
            vd.append(dv)
        else:
            w = prep[l]
            q, k, v, r, ga = _gla_in(x, g_pre, w, tm)
            if sample is None:
                o, s_fin = _gla(q, k, v, ga, None, i, 64, 16, _tile(t, 256))
            else:
                nseq = sample['nseq']
                t_new = t // nseq
                rs = lambda a: a.reshape(nseq, t_new, a.shape[-1])
                o, s_fin = _gla(rs(q), rs(k), rs(v), rs(ga), sample['state_gla'], i, t_new, t_new, t_new)
                o = o.reshape(1, t, o.shape[-1])
            x = _gla_out(x, o, r, w['g_out'], w['w_out'], g_post, tm)
            gla.append(s_fin)
        prev = None if sample is None else sample['state_ffn_conv'][l]
        x, c = _ffn(x, W['g_ffn_pre'][l][None], W['g_ffn_post'][l][None], prep['w_up'][l], W['w_ffn_conv'][l],
                    W['b_ffn_conv'][l][None], prep['w_down'][l], prev, tm)
        conv.append(c)
    return x, jnp.stack(lat), jnp.stack(kro), jnp.stack(kd), jnp.stack(vd), jnp.stack(gla), jnp.stack(conv)


def kernel(x_prompt, x_sample, cache_mla_latent, cache_mla_rope, cache_diff_k, cache_diff_v, state_gla, state_ffn_conv, page_table, g_mix_pre, g_mix_post, g_ffn_pre, g_ffn_post, w_attn_in, g_q_norm, w_uq, g_kv_norm, w_uk, w_uv, lambda_q1, lambda_k1, lambda_q2, lambda_k2, g_diff_subln, w_attn_out, w_gla_in, w_gla_gate, b_gla_gate, g_gla_out, w_gla_out, w_ffn_up, w_ffn_conv, b_ffn_conv, w_ffn_down):
    W = dict(g_mix_pre=g_mix_pre, g_mix_post=g_mix_post, g_ffn_pre=g_ffn_pre, g_ffn_post=g_ffn_post,
             w_attn_in=w_attn_in, g_q_norm=g_q_norm, w_uq=w_uq, g_kv_norm=g_kv_norm, w_uk=w_uk, w_uv=w_uv,
             lambda_q1=lambda_q1, lambda_k1=lambda_k1, lambda_q2=lambda_q2, lambda_k2=lambda_k2,
             g_diff_subln=g_diff_subln, w_attn_out=w_attn_out, w_gla_in=w_gla_in, w_gla_gate=w_gla_gate,
             b_gla_gate=b_gla_gate, g_gla_out=g_gla_out, w_gla_out=w_gla_out, w_ffn_conv=w_ffn_conv,
             b_ffn_conv=b_ffn_conv)
    prep = {l: (_prep_attn(l // 2, W) if l % 2 == 0 else _prep_gla(l // 2, W)) for l in range(DEPTH)}
    prep['w_up'] = w_ffn_up.astype(BF16)
    prep['w_down'] = w_ffn_down.astype(BF16)

    bp, sp, d = x_prompt.shape
    bs, ts, _ = x_sample.shape
    n_pages = page_table.shape[1]
    past = n_pages * PAGE

    yp, p_lat, p_rope, p_k, p_v, p_gla, p_conv = _trunk(x_prompt, _rope_tables(jnp.arange(sp)), W, prep, None)

    la, n_pool = cache_mla_latent.shape[:2]
    sample = dict(
        lat=cache_mla_latent,
        ropeT=jnp.swapaxes(cache_mla_rope, 2, 3),
        dkT=jnp.transpose(cache_diff_k, (0, 1, 3, 4, 5, 2)),
        dv=cache_diff_v.reshape(la, n_pool, PAGE * DIFF_HEADS, DIFF_V),
        page_table=page_table, pps=8 if n_pages % 8 == 0 else 1, nseq=bs,
        state_gla=state_gla, state_ffn_conv=state_ffn_conv)
    tab_s = jnp.tile(_rope_tables(past + jnp.arange(ts)), (bs, 1))
    ys, s_lat, s_rope, s_k, s_v, s_gla, s_conv = _trunk(x_sample.reshape(1, bs * ts, d), tab_s, W, prep, sample)

    def k6(a, b, t):
        return a.reshape(a.shape[0], b, t, DIFF_HEADS, 2, DIFF_DH)

    def v5(a, b, t):
        return a.reshape(a.shape[0], b, t, DIFF_HEADS, DIFF_V)

    return (yp, ys.reshape(bs, ts, d),
            p_lat, p_rope, k6(p_k, bp, sp), v5(p_v, bp, sp), p_gla, p_conv,
            s_lat.reshape(-1, bs, ts, KV_LORA), s_rope.reshape(-1, bs, ts, MLA_ROPE), k6(s_k, bs, ts), v5(s_v, bs, ts),
            s_gla.reshape(-1, bs, GLA_HEADS, GLA_DK, GLA_DV), s_conv)
```

```python
import functools
import math

import jax
import jax.numpy as jnp
from jax import lax
from jax.experimental import pallas as pl
from jax.experimental.pallas import tpu as pltpu

F32 = jnp.float32
BF16 = jnp.bfloat16

D_MODEL = 1024
DEPTH = 4
PAGE = 128
MLA_HEADS = 8
MLA_NOPE = 64
MLA_ROPE = 32
MLA_V = 64
Q_LORA = 256
KV_LORA = 128
DIFF_HEADS = 4
DIFF_DH = 64
DIFF_V = 128
GLA_HEADS = 4
GLA_DK = 128
GLA_DV = 256
GLA_RANK = 16
GLA_TAU = 16.0
D_FF = 2816
ROPE_THETA = 10000.0
EPS = 1e-6
NEG = -1e30
LOG2E = math.log2(math.e)
MLA_SCALE = (MLA_NOPE + MLA_ROPE) ** -0.5 * LOG2E
DIFF_SCALE = DIFF_DH ** -0.5 * LOG2E

LANES = 128
VMEM_LIMIT = 56 * 1024 * 1024

A_CQ, A_CKV, A_DQ, A_DK, A_DV, A_KR, A_W = 0, 256, 384, 896, 1408, 1920, 2048
G_Q, G_K, G_V, G_R, G_A, G_W = 0, 512, 1024, 2048, 3072, 3200

NT = (((1,), (1,)), ((), ()))
TN = (((0,), (0,)), ((), ()))


def _cp(*sem):
    return pltpu.CompilerParams(dimension_semantics=sem, vmem_limit_bytes=VMEM_LIMIT)


def _rms(x, g):
    return x * lax.rsqrt(jnp.mean(x * x, axis=-1, keepdims=True) + EPS) * g


def _dot(a, b):
    return jnp.dot(a, b, preferred_element_type=F32)


def _const_spec(shape):
    nd = len(shape)
    return pl.BlockSpec(shape, lambda *_: (0,) * nd)


def _rope_tables(pos):
    pos = pos.astype(F32)[:, None]
    lane = jnp.arange(LANES)

    def one(d):
        half = d // 2
        inv = ROPE_THETA ** (-jnp.arange(half, dtype=F32) * (2.0 / d))
        ang = pos * inv
        cos = jnp.tile(jnp.cos(ang), (1, LANES // half))
        sin = jnp.tile(jnp.sin(ang), (1, LANES // half))
        first = (lane % d) < half
        return [cos, jnp.where(first, -sin, 0.0), jnp.where(first, 0.0, sin)]

    return jnp.concatenate(one(DIFF_DH) + one(MLA_ROPE), axis=1)


def _rope(x, cos, sin_a, sin_b, half):
    return x * cos + pltpu.roll(x, LANES - half, 1) * sin_a + pltpu.roll(x, half, 1) * sin_b


def _attn_in_kernel(x_ref, g_ref, win_ref, gq_ref, wuq_ref, gkv_ref, wuk_ref, tab_ref, *rest):
    qm_ref, kcat_ref, dq_ref, dkb_ref, dvb_ref, ckv_ref, kr_ref, dkt_ref, dv4_ref = rest[-9:]
    tm = x_ref.shape[0]
    h = _rms(x_ref[...], g_ref[...]).astype(BF16)
    p = _dot(h, win_ref[...])
    tab = tab_ref[...]
    c64, a64, b64, c32, a32, b32 = [tab[:, i * LANES:(i + 1) * LANES] for i in range(6)]

    cqn = _rms(p[:, A_CQ:A_CKV], gq_ref[...]).astype(BF16)
    q = _dot(cqn, wuq_ref[...])
    nope_w = MLA_HEADS * MLA_NOPE
    qlat = _dot(q[:, :nope_w].astype(BF16), wuk_ref[...])
    for hh in range(MLA_HEADS):
        ql = qlat[:, hh * LANES:(hh + 1) * LANES]
        qr = _rope(q[:, nope_w + hh * LANES:nope_w + (hh + 1) * LANES], c32, a32, b32, MLA_ROPE // 2)
        qm_ref[hh] = (jnp.concatenate([ql, qr], axis=1) * MLA_SCALE).astype(qm_ref.dtype)

    ckv = _rms(p[:, A_CKV:A_DQ], gkv_ref[...])
    ckv_ref[...] = ckv
    krs = _rope(p[:, A_KR:A_W], c32, a32, b32, MLA_ROPE // 2)
    kr_ref[...] = krs[:, :MLA_ROPE]
    kcat_ref[...] = jnp.concatenate([ckv, krs], axis=1).astype(kcat_ref.dtype)
    for s in range(4):
        sl = slice(s * LANES, (s + 1) * LANES)
        dq = _rope(p[:, A_DQ + s * LANES:A_DQ + (s + 1) * LANES], c64, a64, b64, DIFF_DH // 2)
        dq_ref[:, sl] = (dq * DIFF_SCALE).astype(dq_ref.dtype)
        dk = _rope(p[:, A_DK + s * LANES:A_DK + (s + 1) * LANES], c64, a64, b64, DIFF_DH // 2)
        dkt_ref[sl, :] = dk.T
        dkb_ref[:, sl] = dk.astype(dkb_ref.dtype)
    dv = p[:, A_DV:A_KR]
    for hh in range(DIFF_HEADS):
        dv4_ref[pl.ds(hh, tm, stride=DIFF_HEADS), :] = dv[:, hh * LANES:(hh + 1) * LANES]
    dvb_ref[...] = dv.astype(dvb_ref.dtype)


def _attn_in(x, g, w, tab, act_dtype, tm, layer, bufs):
    nb, t, d = x.shape
    n_layers = bufs[0].shape[0]
    grid = (nb, t // tm)
    row = lambda c: pl.BlockSpec((None, tm, c), lambda b, i: (b, i, 0))
    lrow = lambda c: pl.BlockSpec((None, None, tm, c), lambda b, i: (layer, b, i, 0))
    out_shape = (
        jax.ShapeDtypeStruct((nb, MLA_HEADS, t, 2 * LANES), act_dtype),
        jax.ShapeDtypeStruct((nb, t, 2 * LANES), act_dtype),
        jax.ShapeDtypeStruct((nb, t, 512), act_dtype),
        jax.ShapeDtypeStruct((nb, t, 512), act_dtype),
        jax.ShapeDtypeStruct((nb, t, 512), act_dtype),
        jax.ShapeDtypeStruct((n_layers, nb, t, KV_LORA), F32),
        jax.ShapeDtypeStruct((n_layers, nb, t, MLA_ROPE), F32),
        jax.ShapeDtypeStruct((n_layers, nb, 512, t), F32),
        jax.ShapeDtypeStruct((n_layers, nb, t * DIFF_HEADS, DIFF_V), F32),
    )
    out_specs = (
        pl.BlockSpec((None, MLA_HEADS, tm, 2 * LANES), lambda b, i: (b, 0, i, 0)),
        row(2 * LANES), row(512), row(512), row(512), lrow(KV_LORA), lrow(MLA_ROPE),
        pl.BlockSpec((None, None, 512, tm), lambda b, i: (layer, b, 0, i)),
        pl.BlockSpec((None, None, tm * DIFF_HEADS, DIFF_V), lambda b, i: (layer, b, i, 0)),
    )
    in_specs = [
        row(d), _const_spec((1, d)), _const_spec(w['w_in'].shape), _const_spec((1, Q_LORA)),
        _const_spec(w['w_uq'].shape), _const_spec((1, KV_LORA)), _const_spec(w['w_ukbd'].shape),
        pl.BlockSpec((tm, 6 * LANES), lambda b, i: (i, 0)),
    ]
    args = [x, g, w['w_in'], w['g_q'], w['w_uq'], w['g_kv'], w['w_ukbd'], tab]
    aliases = {}
    for j, buf in enumerate(bufs):
        assert buf.shape == out_shape[5 + j].shape
        aliases[len(args)] = 5 + j
        in_specs.append(pl.BlockSpec(memory_space=pl.ANY))
        args.append(buf)
    return pl.pallas_call(
        _attn_in_kernel, grid=grid, in_specs=in_specs, out_specs=out_specs, out_shape=out_shape,
        input_output_aliases=aliases, compiler_params=_cp("parallel", "parallel"), name="attn_in",
    )(*args)


def _online_update(s, v, m_ref, l_ref, acc_ref):
    m_prev = m_ref[...]
    m_new = jnp.maximum(m_prev, jnp.max(s, axis=1, keepdims=True))
    alpha = jnp.exp2(m_prev - m_new)
    p = jnp.exp2(s - m_new)
    l_ref[...] = alpha * l_ref[...] + jnp.sum(p, axis=1, keepdims=True)
    acc_ref[...] = alpha * acc_ref[...] + _dot(p.astype(v.dtype), v)
    m_ref[...] = m_new


def _lambda(lam_ref, lam_init):
    lv = lam_ref[...]
    s1 = jnp.sum(lv[0:1] * lv[1:2], axis=1, keepdims=True)
    s2 = jnp.sum(lv[2:3] * lv[3:4], axis=1, keepdims=True)
    return jnp.exp(s1) - jnp.exp(s2) + lam_init


def _split_maps(dqh):
    lane = lax.broadcasted_iota(jnp.int32, dqh.shape, 1)
    zero = jnp.zeros_like(dqh)
    return jnp.concatenate([jnp.where(lane < DIFF_DH, dqh, zero), jnp.where(lane >= DIFF_DH, dqh, zero)], axis=0)


FLASH_TILES = MLA_HEADS + 2 * DIFF_HEADS
FLASH_KCHUNK = 512
FLASH_AHEAD = 2


def _flash_kernel(qm_ref, kcat_ref, dq_ref, dk_ref, dv_ref, lam_ref, olat_ref, odiff_ref,
                  dqs, m_ref, l_ref, acc_ref, *, tq, lam_init):
    qi = pl.program_id(1)
    m_ref[...] = jnp.full(m_ref.shape, NEG, F32)
    l_ref[...] = jnp.zeros(l_ref.shape, F32)
    acc_ref[...] = jnp.zeros(acc_ref.shape, F32)
    lane = lax.broadcasted_iota(jnp.int32, (tq, LANES), 1)
    for h in range(DIFF_HEADS):
        dqh = dq_ref[:, h * LANES:(h + 1) * LANES]
        zero = jnp.zeros_like(dqh)
        dqs[2 * h] = jnp.where(lane < DIFF_DH, dqh, zero)
        dqs[2 * h + 1] = jnp.where(lane >= DIFF_DH, dqh, zero)

    def chunk(k0, w, masked):
        kc = kcat_ref[pl.ds(k0, w), :]
        nl = w // LANES

        def scores(t):
            if t < MLA_HEADS:
                q, kk = qm_ref[t], kc
            else:
                h = (t - MLA_HEADS) // 2
                q = dqs[t - MLA_HEADS]
                kk = dk_ref[pl.ds(k0, w), h * LANES:(h + 1) * LANES]
            s = lax.dot_general(q, kk, NT, preferred_element_type=F32)
            if masked:
                row = lax.broadcasted_iota(jnp.int32, s.shape, 0)
                col = lax.broadcasted_iota(jnp.int32, s.shape, 1)
                s = jnp.where(col <= row, s, NEG)
            return s

        def update(t, s):
            if t < MLA_HEADS:
                v = kc[:, :KV_LORA]
            else:
                h = (t - MLA_HEADS) // 2
                v = dv_ref[pl.ds(k0, w), h * LANES:(h + 1) * LANES]
            mx = s[:, :LANES]
            for j in range(1, nl):
                mx = jnp.maximum(mx, s[:, j * LANES:(j + 1) * LANES])
            m_prev = m_ref[t]
            m_new = jnp.maximum(m_prev, jnp.max(mx, axis=1, keepdims=True))
            alpha = jnp.exp2(m_prev - m_new)
            ps = None
            pbs = []
            for j in range(nl):
                p = jnp.exp2(s[:, j * LANES:(j + 1) * LANES] - m_new)
                ps = p if ps is None else ps + p
                pbs.append(p.astype(BF16))
            l_ref[t] = alpha * l_ref[t] + ps
            acc_ref[t] = alpha * acc_ref[t] + _dot(jnp.concatenate(pbs, axis=1), v)
            m_ref[t] = m_new

        pend = [scores(t) for t in range(FLASH_AHEAD)]
        for t in range(FLASH_TILES):
            if t + FLASH_AHEAD < FLASH_TILES:
                pend.append(scores(t + FLASH_AHEAD))
            update(t, pend.pop(0))

    def body(j, carry):
        chunk(pl.multiple_of(j * FLASH_KCHUNK, FLASH_KCHUNK), FLASH_KCHUNK, False)
        return carry

    lax.fori_loop(0, (qi * tq) // FLASH_KCHUNK, body, 0)
    per = FLASH_KCHUNK // tq
    for r in range(1, per):
        @pl.when(qi % per >= r)
        def _():
            chunk(pl.multiple_of((qi - qi % per + (r - 1)) * tq, tq), tq, False)
    chunk(pl.multiple_of(qi * tq, tq), tq, True)

    lam = _lambda(lam_ref, lam_init)
    outs = [acc_ref[t] * (1.0 / jnp.sum(l_ref[t], axis=1, keepdims=True)) for t in range(FLASH_TILES)]
    for h in range(MLA_HEADS):
        olat_ref[:, h * LANES:(h + 1) * LANES] = outs[h].astype(olat_ref.dtype)
    for h in range(DIFF_HEADS):
        odiff_ref[:, h * LANES:(h + 1) * LANES] = outs[MLA_HEADS + 2 * h] - lam * outs[MLA_HEADS + 2 * h + 1]


def _flash(qm, kcat, dq, dk, dv, lam, lam_init, tq):
    nb, _, s, _ = qm.shape
    assert FLASH_KCHUNK % tq == 0 and s % tq == 0
    whole = lambda w: pl.BlockSpec((None, s, w), lambda b, qi: (b, 0, 0))
    in_specs = [
        pl.BlockSpec((None, MLA_HEADS, tq, 2 * LANES), lambda b, qi: (b, 0, qi, 0)),
        whole(2 * LANES),
        pl.BlockSpec((None, tq, 512), lambda b, qi: (b, qi, 0)),
        whole(512), whole(512), _const_spec((8, LANES)),
    ]
    out_specs = (pl.BlockSpec((None, tq, MLA_HEADS * LANES), lambda b, qi: (b, qi, 0)),
                 pl.BlockSpec((None, tq, 512), lambda b, qi: (b, qi, 0)))
    out_shape = (jax.ShapeDtypeStruct((nb, s, MLA_HEADS * LANES), BF16), jax.ShapeDtypeStruct((nb, s, 512), F32))
    scratch = [pltpu.VMEM((2 * DIFF_HEADS, tq, LANES), BF16), pltpu.VMEM((FLASH_TILES, tq, LANES), F32),
               pltpu.VMEM((FLASH_TILES, tq, LANES), F32), pltpu.VMEM((FLASH_TILES, tq, LANES), F32)]
    return pl.pallas_call(
        functools.partial(_flash_kernel, tq=tq, lam_init=lam_init), grid=(nb, s // tq), in_specs=in_specs,
        out_specs=out_specs, out_shape=out_shape, scratch_shapes=scratch,
        compiler_params=_cp("parallel", "arbitrary"), name="flash_prompt",
    )(qm, kcat, dq, dk, dv, lam)


def _decode_kernel(pt_ref, qm_ref, dq_ref, kcat_ref, dkn_ref, dvn_ref, lam_ref, *rest, pps, t_new, lam_init):
    lat_refs = rest[0 * pps:1 * pps]
    rope_refs = rest[1 * pps:2 * pps]
    dk_refs = rest[2 * pps:3 * pps]
    dv_refs = rest[3 * pps:4 * pps]
    olat_ref, odiff_ref, qs, m1, l1, acc1, m2, l2, acc2 = rest[4 * pps:]
    c = pl.program_id(1)
    nc = pl.num_programs(1)
    rows = MLA_HEADS * t_new

    @pl.when(c == 0)
    def _():
        qs[...] = qm_ref[...].reshape(rows, 2 * LANES)
        m1[...] = jnp.full(m1.shape, NEG, F32)
        l1[...] = jnp.zeros(l1.shape, F32)
        acc1[...] = jnp.zeros(acc1.shape, F32)
        m2[...] = jnp.full(m2.shape, NEG, F32)
        l2[...] = jnp.zeros(l2.shape, F32)
        acc2[...] = jnp.zeros(acc2.shape, F32)

    q = qs[...]
    ql = q[:, :KV_LORA].astype(BF16)
    qr = q[:, KV_LORA:KV_LORA + MLA_ROPE].astype(BF16)
    dq = dq_ref[...]

    lat = [r[...].astype(BF16) for r in lat_refs]
    scores = [jnp.concatenate(
        [lax.dot_general(ql, lat[j], NT, preferred_element_type=F32) + _dot(qr, rope_refs[j][...].astype(BF16))
         for j in range(pps)], axis=1)]
    for h in range(DIFF_HEADS):
        q1 = dq[:, h * LANES:h * LANES + DIFF_DH].astype(BF16)
        q2 = dq[:, h * LANES + DIFF_DH:(h + 1) * LANES].astype(BF16)
        scores.append(jnp.concatenate(
            [jnp.concatenate([_dot(q1, dk_refs[j][h, 0].astype(BF16)), _dot(q2, dk_refs[j][h, 1].astype(BF16))], axis=0)
             for j in range(pps)], axis=1))

    def softmax_pv(s, values, m_ref, l_ref, acc_ref):
        m_prev = m_ref[...]
        m_new = jnp.maximum(m_prev, jnp.max(s, axis=1, keepdims=True))
        alpha = jnp.exp2(m_prev - m_new)
        p = jnp.exp2(s - m_new)
        l_ref[...] = alpha * l_ref[...] + jnp.sum(p, axis=1, keepdims=True)
        pb = p.astype(BF16)
        pv = None
        for j in range(pps):
            t = _dot(pb[:, j * PAGE:(j + 1) * PAGE], values(j))
            pv = t if pv is None else pv + t
        acc_ref[...] = alpha * acc_ref[...] + pv
        m_ref[...] = m_new

    softmax_pv(scores[0], lambda j: lat[j], m1, l1, acc1)
    for h in range(DIFF_HEADS):
        softmax_pv(scores[1 + h], lambda j: dv_refs[j][pl.ds(h, PAGE, stride=DIFF_HEADS), :].astype(BF16),
                   m2.at[h], l2.at[h], acc2.at[h])

    @pl.when(c == nc - 1)
    def _():
        kc = kcat_ref[...]
        s = lax.dot_general(q, kc, NT, preferred_element_type=F32)
        row = lax.broadcasted_iota(jnp.int32, s.shape, 0) & (t_new - 1)
        col = lax.broadcasted_iota(jnp.int32, s.shape, 1)
        s = jnp.where(col <= row, s, NEG)
        _online_update(s, kc[:, :KV_LORA], m1, l1, acc1)
        o = acc1[...] / l1[...]
        for h in range(MLA_HEADS):
            olat_ref[:, h * LANES:(h + 1) * LANES] = o[h * t_new:(h + 1) * t_new].astype(olat_ref.dtype)
        lam = _lambda(lam_ref, lam_init)
        dkn = dkn_ref[...]
        dvn = dvn_ref[...]
        for h in range(DIFF_HEADS):
            sl = slice(h * LANES, (h + 1) * LANES)
            qq = _split_maps(dq[:, sl])
            s = lax.dot_general(qq, dkn[:, sl], NT, preferred_element_type=F32)
            row = lax.broadcasted_iota(jnp.int32, s.shape, 0) & (t_new - 1)
            col = lax.broadcasted_iota(jnp.int32, s.shape, 1)
            s = jnp.where(col <= row, s, NEG)
            _online_update(s, dvn[:, sl], m2.at[h], l2.at[h], acc2.at[h])
            o = acc2[h] / l2[h]
            odiff_ref[:, sl] = o[:t_new] - lam * o[t_new:]


def _decode(layer, qm, dq, kcat, dkn, dvn, lam, lam_init, lat_pool, ropeT_pool, dkT_pool, dv_pool, page_table, pps):
    _, _, tot, _ = qm.shape
    nbatch, n_pages = page_table.shape
    t_new = tot // nbatch
    nc = n_pages // pps
    rows = MLA_HEADS * t_new

    def page(j):
        return lambda b, c, pt: (layer, pt[b, c * pps + j]) + (0,) * 2

    def page4(j):
        return lambda b, c, pt: (layer, pt[b, c * pps + j], 0, 0, 0, 0)

    new = lambda w: pl.BlockSpec((None, t_new, w), lambda b, c, pt: (0, b, 0))
    in_specs = [
        pl.BlockSpec((None, MLA_HEADS, t_new, 2 * LANES), lambda b, c, pt: (0, 0, b, 0)),
        new(512), new(2 * LANES), new(512), new(512),
        pl.BlockSpec((8, LANES), lambda b, c, pt: (0, 0)),
    ]
    in_specs += [pl.BlockSpec((None, None, PAGE, KV_LORA), page(j)) for j in range(pps)]
    in_specs += [pl.BlockSpec((None, None, MLA_ROPE, PAGE), page(j)) for j in range(pps)]
    in_specs += [pl.BlockSpec((None, None, DIFF_HEADS, 2, DIFF_DH, PAGE), page4(j)) for j in range(pps)]
    in_specs += [pl.BlockSpec((None, None, PAGE * DIFF_HEADS, DIFF_V), page(j)) for j in range(pps)]
    out_specs = (new(MLA_HEADS * LANES), new(512))
    out_shape = (jax.ShapeDtypeStruct((1, tot, MLA_HEADS * LANES), F32),
                 jax.ShapeDtypeStruct((1, tot, 512), F32))
    scratch = [
        pltpu.VMEM((rows, 2 * LANES), F32),
        pltpu.VMEM((rows, 1), F32), pltpu.VMEM((rows, 1), F32), pltpu.VMEM((rows, KV_LORA), F32),
        pltpu.VMEM((DIFF_HEADS, 2 * t_new, 1), F32), pltpu.VMEM((DIFF_HEADS, 2 * t_new, 1), F32),
        pltpu.VMEM((DIFF_HEADS, 2 * t_new, DIFF_V), F32),
    ]
    grid_spec = pltpu.PrefetchScalarGridSpec(
        num_scalar_prefetch=1, grid=(nbatch, nc), in_specs=in_specs, out_specs=out_specs, scratch_shapes=scratch)
    return pl.pallas_call(
        functools.partial(_decode_kernel, pps=pps, t_new=t_new, lam_init=lam_init),
        grid_spec=grid_spec, out_shape=out_shape,
        compiler_params=_cp("parallel", "arbitrary"), name="decode_attn",
    )(page_table, qm, dq, kcat, dkn, dvn, lam,
      *([lat_pool] * pps), *([ropeT_pool] * pps), *([dkT_pool] * pps), *([dv_pool] * pps))


def _attn_out_kernel(x_ref, olat_ref, odiff_ref, wuv_ref, gsub_ref, wo_ref, gpost_ref, o_ref, *, scale):
    o_mla = _dot(olat_ref[...].astype(BF16), wuv_ref[...])
    od = odiff_ref[...]
    parts = [o_mla]
    for h in range(DIFF_HEADS):
        parts.append(_rms(od[:, h * LANES:(h + 1) * LANES], gsub_ref[...]) * scale)
    mix = _dot(jnp.concatenate(parts, axis=1).astype(BF16), wo_ref[...])
    o_ref[...] = x_ref[...] + _rms(mix, gpost_ref[...])


def _attn_out(x, olat, odiff, w, gpost, scale, tm):
    nb, t, d = x.shape
    row = lambda c: pl.BlockSpec((None, tm, c), lambda b, i: (b, i, 0))
    in_specs = [row(d), row(MLA_HEADS * LANES), row(512), _const_spec(w['w_uvbd'].shape), _const_spec((1, DIFF_V)),
                _const_spec(w['w_out'].shape), _const_spec((1, d))]
    return pl.pallas_call(
        functools.partial(_attn_out_kernel, scale=scale), grid=(nb, t // tm), in_specs=in_specs,
        out_specs=row(d), out_shape=jax.ShapeDtypeStruct(x.shape, F32),
        compiler_params=_cp("parallel", "parallel"), name="attn_out",
    )(x, olat, odiff, w['w_uvbd'], w['g_sub'], w['w_out'], gpost)


FFN_CHUNK = 256


def _ffn_kernel(*refs, period, has_prev):
    if has_prev:
        x_ref, gpre_ref, gpost_ref, wup_ref, wconv_ref, bconv_ref, wdown_ref, prev_ref, o_ref, st_ref, acc, carry = refs
    else:
        x_ref, gpre_ref, gpost_ref, wup_ref, wconv_ref, bconv_ref, wdown_ref, o_ref, st_ref, acc, carry = refs
    tm = x_ref.shape[0]
    i = pl.program_id(1)
    x = x_ref[...]
    h = _rms(x, gpre_ref[...]).astype(BF16)
    row = lax.broadcasted_iota(jnp.int32, (tm, FFN_CHUNK), 0)
    t = row & (period - 1) if period < tm else row

    if not has_prev:
        @pl.when(i == 0)
        def _():
            carry[...] = jnp.zeros(carry.shape, F32)

    def up(c):
        return (_dot(h, wup_ref[:, c * FFN_CHUNK:(c + 1) * FFN_CHUNK]),
                _dot(h, wup_ref[:, D_FF + c * FFN_CHUNK:D_FF + (c + 1) * FFN_CHUNK]))

    nchunk = D_FF // FFN_CHUNK
    acc[...] = jnp.zeros(acc.shape, F32)
    nxt = up(0)
    for c in range(nchunk):
        sl = slice(c * FFN_CHUNK, (c + 1) * FFN_CHUNK)
        a, v = nxt
        if c + 1 < nchunk:
            nxt = up(c + 1)
        if has_prev:
            nseq = tm // period
            pr = prev_ref[:, :, sl]
            p0 = jnp.broadcast_to(pr[:, 0:1, :], (nseq, period, FFN_CHUNK)).reshape(tm, FFN_CHUNK)
            p1 = jnp.broadcast_to(pr[:, 1:2, :], (nseq, period, FFN_CHUNK)).reshape(tm, FFN_CHUNK)
            st_ref[:, :, sl] = a.reshape(nseq, period, FFN_CHUNK)[:, period - 2:, :]
        else:
            cr = carry[:, sl]
            p0 = cr[6:7, :]
            p1 = cr[7:8, :]
            carry[:, sl] = a[tm - 8:, :]
            st_ref[:, sl] = a[tm - 2:, :]
        am1 = jnp.where(t == 0, p1, pltpu.roll(a, 1, 0))
        am2 = jnp.where(t == 0, p0, jnp.where(t == 1, p1, pltpu.roll(a, 2, 0)))
        wc = wconv_ref[:, sl]
        ac = bconv_ref[:, sl] + wc[0:1] * am2 + wc[1:2] * am1 + wc[2:3] * a
        gelu = 0.5 * ac * (1.0 + lax.erf(ac * math.sqrt(0.5)))
        z = (gelu * v).astype(BF16)
        acc[...] += _dot(z, wdown_ref[sl, :])
    o_ref[...] = x + _rms(acc[...], gpost_ref[...])


def _ffn(x, gpre, gpost, w_up, w_conv, b_conv, w_down, prev, tm):
    nb, t, d = x.shape
    has_prev = prev is not None
    row = pl.BlockSpec((None, tm, d), lambda b, i: (b, i, 0))
    in_specs = [row, _const_spec((1, d)), _const_spec((1, d)), _const_spec(w_up.shape), _const_spec((3, D_FF)),
                _const_spec((1, D_FF)), _const_spec(w_down.shape)]
    args = [x, gpre, gpost, w_up, w_conv, b_conv, w_down]
    if has_prev:
        nseq = prev.shape[0]
        period = t // nseq
        spt = tm // period
        in_specs.append(pl.BlockSpec((spt, 2, D_FF), lambda b, i: (i, 0, 0)))
        args.append(prev)
        st_spec = pl.BlockSpec((spt, 2, D_FF), lambda b, i: (i, 0, 0))
        st_shape = jax.ShapeDtypeStruct((nseq, 2, D_FF), F32)
    else:
        period = t
        st_spec = pl.BlockSpec((None, 2, D_FF), lambda b, i: (b, 0, 0))
        st_shape = jax.ShapeDtypeStruct((nb, 2, D_FF), F32)
    return pl.pallas_call(
        functools.partial(_ffn_kernel, period=period, has_prev=has_prev), grid=(nb, t // tm), in_specs=in_specs,
        out_specs=(row, st_spec), out_shape=(jax.ShapeDtypeStruct(x.shape, F32), st_shape),
        scratch_shapes=[pltpu.VMEM((tm, d), F32), pltpu.VMEM((8, D_FF), F32)],
        compiler_params=_cp("parallel", "arbitrary"), name="conv_ffn",
    )(*args)


def _gla_in_kernel(x_ref, g_ref, win_ref, wg_ref, bg_ref, q_ref, k_ref, v_ref, r_ref, ga_ref):
    h = _rms(x_ref[...], g_ref[...]).astype(BF16)
    p = _dot(h, win_ref[...])
    q_ref[...] = p[:, G_Q:G_K] * (GLA_DK ** -0.5)
    k_ref[...] = p[:, G_K:G_V]
    v_ref[...] = p[:, G_V:G_R].astype(v_ref.dtype)
    r_ref[...] = p[:, G_R:G_A]
    z = _dot(p[:, G_A:G_W].astype(BF16), wg_ref[...]) + bg_ref[...]
    ga_ref[...] = (jnp.minimum(z, 0.0) - jnp.log1p(jnp.exp(-jnp.abs(z)))) * (1.0 / GLA_TAU)


def _gla_in(x, g, w, tm):
    nb, t, d = x.shape
    row = lambda c: pl.BlockSpec((None, tm, c), lambda b, i: (b, i, 0))
    sds = lambda c, dt: jax.ShapeDtypeStruct((nb, t, c), dt)
    return pl.pallas_call(
        _gla_in_kernel, grid=(nb, t // tm),
        in_specs=[row(d), _const_spec((1, d)), _const_spec(w['w_in'].shape), _const_spec(w['w_gate'].shape),
                  _const_spec((1, 512))],
        out_specs=(row(512), row(512), row(1024), row(1024), row(512)),
        out_shape=(sds(512, F32), sds(512, F32), sds(1024, BF16), sds(1024, F32), sds(512, F32)),
        compiler_params=_cp("parallel", "parallel"), name="gla_in",
    )(x, g, w['w_in'], w['w_gate'], w['b_gate'])


def _gla_chunk(q_ref, k_ref, v_ref, g_ref, b_ref, base, head, c, state, sub):
    mm = BF16 if c >= 16 else F32
    kcol = slice(head * GLA_DK, (head + 1) * GLA_DK)
    q = q_ref[base:base + c, kcol]
    k = k_ref[base:base + c, kcol]
    v = v_ref[base:base + c, head * GLA_DV:(head + 1) * GLA_DV].astype(mm)
    row = lax.broadcasted_iota(jnp.int32, (c, GLA_DK), 0)
    b = g_ref[base:base + c, kcol]
    step = 1
    while step < c:
        b = b + jnp.where(row >= step, pltpu.roll(b, step, 0), 0.0)
        step *= 2
    b_ref[...] = b
    b_end = b_ref[c - 1:c, :]
    lane_c = lax.broadcasted_iota(jnp.int32, (sub, c), 1)
    rsub = lax.broadcasted_iota(jnp.int32, (sub, c), 0)
    blocks = []
    for i in range(c // sub):
        r0 = i * sub
        qi = q[r0:r0 + sub]
        bi = b[r0:r0 + sub]
        if i == 0:
            a_i = jnp.zeros((sub, c), F32)
        else:
            beta = b_ref[r0 - 1:r0, :]
            qt = (qi * jnp.exp(bi - beta)).astype(mm)
            kt = (k * jnp.exp(jnp.minimum(beta - b, 0.0))).astype(mm)
            a_i = lax.dot_general(qt, kt, NT, preferred_element_type=F32)
            a_i = jnp.where(lane_c < r0, a_i, 0.0)
        for s in range(sub):
            r = r0 + s
            ks = k_ref[base + r:base + r + 1, kcol]
            bs = b_ref[r:r + 1, :]
            w = qi * ks * jnp.exp(jnp.minimum(bi - bs, 0.0))
            col = jnp.sum(w, axis=1, keepdims=True)
            a_i = jnp.where(jnp.logical_and(lane_c == r, rsub >= s), col, a_i)
        blocks.append(a_i)
    a = blocks[0] if len(blocks) == 1 else jnp.concatenate(blocks, axis=0)
    qd = (q * jnp.exp(b)).astype(mm)
    o = _dot(qd, state.astype(mm)) + _dot(a.astype(mm), v)
    kd = (k * jnp.exp(b_end - b)).astype(mm)
    rr = lax.broadcasted_iota(jnp.int32, (GLA_DK, GLA_DK), 0)
    cc = lax.broadcasted_iota(jnp.int32, (GLA_DK, GLA_DK), 1)
    decay = jnp.sum(jnp.where(rr == cc, jnp.exp(b_end), 0.0), axis=1, keepdims=True)
    new_state = decay * state + lax.dot_general(kd, v, TN, preferred_element_type=F32)
    return o, new_state


def _gla_kernel(*refs, chunk, sub, nseq, has_s0):
    if has_s0:
        q_ref, k_ref, v_ref, g_ref, s0_ref, _, o_ref, s_ref, st, b_scr = refs
    else:
        q_ref, k_ref, v_ref, g_ref, _, o_ref, s_ref, st, b_scr = refs

        @pl.when(pl.program_id(1) == 0)
        def _():
            st[...] = jnp.zeros(st.shape, F32)

    rows = q_ref.shape[0] // nseq
    for j in range(nseq):
        states = [s0_ref[j, h] if has_s0 else st[h] for h in range(GLA_HEADS)]
        for n in range(rows // chunk):
            base = j * rows + n * chunk
            for h in range(GLA_HEADS):
                o, states[h] = _gla_chunk(q_ref, k_ref, v_ref, g_ref, b_scr.at[h], base, h, chunk, states[h], sub)
                o_ref[base:base + chunk, h * GLA_DV:(h + 1) * GLA_DV] = o
        for h in range(GLA_HEADS):
            if not has_s0:
                st[h] = states[h]
            s_ref[j, h] = states[h]


def _gla(q, k, v, g, s0, s_buf, layer, chunk, sub, tt, nseq):
    nb, t, _ = q.shape
    has_s0 = s0 is not None
    assert (nb == 1) if has_s0 else (nseq == 1)
    blk = lambda w: pl.BlockSpec((None, tt, w), lambda b, i: (b, i, 0))
    st_blk = (None, nseq, GLA_HEADS, GLA_DK, GLA_DV)
    in_specs = [blk(GLA_HEADS * GLA_DK), blk(GLA_HEADS * GLA_DK), blk(GLA_HEADS * GLA_DV), blk(GLA_HEADS * GLA_DK)]
    args = [q, k, v, g]
    if has_s0:
        in_specs.append(pl.BlockSpec(st_blk, lambda b, i: (layer, i, 0, 0, 0)))
        args.append(s0)
        s_spec = pl.BlockSpec(st_blk, lambda b, i: (layer, i, 0, 0, 0))
        assert s_buf.shape[1] == (t // tt) * nseq
    else:
        s_spec = pl.BlockSpec(st_blk, lambda b, i: (layer, b, 0, 0, 0))
        assert s_buf.shape[1] == nb
    in_specs.append(pl.BlockSpec(memory_space=pl.ANY))
    args.append(s_buf)
    return pl.pallas_call(
        functools.partial(_gla_kernel, chunk=chunk, sub=sub, nseq=nseq, has_s0=has_s0),
        grid=(nb, t // tt), in_specs=in_specs, out_specs=(blk(GLA_HEADS * GLA_DV), s_spec),
        out_shape=(jax.ShapeDtypeStruct((nb, t, GLA_HEADS * GLA_DV), F32),
                   jax.ShapeDtypeStruct(s_buf.shape, F32)),
        input_output_aliases={len(args) - 1: 1},
        scratch_shapes=[pltpu.VMEM((GLA_HEADS, GLA_DK, GLA_DV), F32), pltpu.VMEM((GLA_HEADS, chunk, GLA_DK), F32)],
        compiler_params=_cp("parallel", "arbitrary"), name="gla_core",
    )(*args)


def _gla_out_kernel(x_ref, o_ref, r_ref, gout_ref, wo_ref, gpost_ref, out_ref):
    o = o_ref[...]
    parts = [_rms(o[:, h * GLA_DV:(h + 1) * GLA_DV], gout_ref[...]) for h in range(GLA_HEADS)]
    y = jnp.concatenate(parts, axis=1) * jax.nn.silu(r_ref[...])
    mix = _dot(y.astype(BF16), wo_ref[...])
    out_ref[...] = x_ref[...] + _rms(mix, gpost_ref[...])


def _gla_out(x, o, r, gout, wo, gpost, tm):
    nb, t, d = x.shape
    row = lambda c: pl.BlockSpec((None, tm, c), lambda b, i: (b, i, 0))
    return pl.pallas_call(
        _gla_out_kernel, grid=(nb, t // tm),
        in_specs=[row(d), row(1024), row(1024), _const_spec((1, GLA_DV)), _const_spec(wo.shape), _const_spec((1, d))],
        out_specs=row(d), out_shape=jax.ShapeDtypeStruct(x.shape, F32),
        compiler_params=_cp("parallel", "parallel"), name="gla_out",
    )(x, o, r, gout, wo, gpost)


def _prep_attn(i, W):
    w_in = W['w_attn_in'][i]
    cq, ckv, kr, dq, dk, dv = jnp.split(w_in, [256, 384, 416, 928, 1440], axis=1)
    w_in2 = jnp.concatenate([cq, ckv, dq, dk, dv, kr, jnp.zeros((D_MODEL, A_W - A_KR - MLA_ROPE), F32)], axis=1)
    wq = W['w_uq'][i].reshape(Q_LORA, MLA_HEADS, MLA_NOPE + MLA_ROPE)
    w_nope = wq[:, :, :MLA_NOPE].reshape(Q_LORA, MLA_HEADS * MLA_NOPE)
    w_rope = jnp.pad(wq[:, :, MLA_NOPE:], ((0, 0), (0, 0), (0, LANES - MLA_ROPE))).reshape(Q_LORA, MLA_HEADS * LANES)
    eye = jnp.eye(MLA_HEADS, dtype=F32)
    w_ukbd = jnp.einsum('chn,hg->hngc', W['w_uk'][i], eye).reshape(MLA_HEADS * MLA_NOPE, MLA_HEADS * KV_LORA)
    w_uvbd = jnp.einsum('chv,hg->hcgv', W['w_uv'][i], eye).reshape(MLA_HEADS * KV_LORA, MLA_HEADS * MLA_V)
    lam = jnp.zeros((8, LANES), F32)
    for r, name in enumerate(('lambda_q1', 'lambda_k1', 'lambda_q2', 'lambda_k2')):
        lam = lam.at[r, :DIFF_DH].set(W[name][i])
    return dict(
        w_in=w_in2.astype(BF16), g_q=W['g_q_norm'][i][None], w_uq=jnp.concatenate([w_nope, w_rope], 1).astype(BF16),
        g_kv=W['g_kv_norm'][i][None], w_ukbd=w_ukbd.astype(BF16), w_uvbd=w_uvbd.astype(BF16),
        g_sub=W['g_diff_subln'][i][None], w_out=W['w_attn_out'][i].astype(BF16), lam=lam)


def _prep_gla(i, W):
    w_in = W['w_gla_in'][i]
    q, k, v, a, r = jnp.split(w_in, [512, 1024, 2048, 2064], axis=1)
    w_in2 = jnp.concatenate([q, k, v, r, a, jnp.zeros((D_MODEL, G_W - G_A - GLA_RANK), F32)], axis=1)
    w_gate = jnp.pad(W['w_gla_gate'][i], ((0, LANES - GLA_RANK), (0, 0)))
    return dict(w_in=w_in2.astype(BF16), w_gate=w_gate.astype(BF16), b_gate=W['b_gla_gate'][i][None],
                g_out=W['g_gla_out'][i][None], w_out=W['w_gla_out'][i].astype(BF16))


def _tile(t, pref):
    return pref if t % pref == 0 else t


def _trunk(x, tab, W, prep, sample):
    nb, t, d = x.shape
    tm = _tile(t, 512)
    act = F32 if sample is not None else BF16
    n_attn, n_gla = (DEPTH + 1) // 2, DEPTH // 2
    caches = (jnp.zeros((n_attn, nb, t, KV_LORA), F32), jnp.zeros((n_attn, nb, t, MLA_ROPE), F32),
              jnp.zeros((n_attn, nb, 512, t), F32), jnp.zeros((n_attn, nb, t * DIFF_HEADS, DIFF_V), F32))
    gla = jnp.zeros((n_gla, nb if sample is None else sample['nseq'], GLA_HEADS, GLA_DK, GLA_DV), F32)
    conv = []
    for l in range(DEPTH):
        i = l // 2
        g_pre = W['g_mix_pre'][l][None]
        g_post = W['g_mix_post'][l][None]
        w = prep[l]
        if l % 2 == 0:
            lam_init = 0.8 - 0.6 * math.exp(-0.3 * l)
            qm, kcat, dq, dkb, dvb, *caches = _attn_in(x, g_pre, w, tab, act, tm, i, caches)
            if sample is None:
                olat, odiff = _flash(qm, kcat, dq, dkb, dvb, w['lam'], lam_init, _tile(t, 256))
            else:
                olat, odiff = _decode(i, qm, dq, kcat, dkb, dvb, w['lam'], lam_init, sample['lat'], sample['ropeT'],
                                      sample['dkT'], sample['dv'], sample['page_table'], sample['pps'])
            x = _attn_out(x, olat, odiff, w, g_post, 1.0 - lam_init, tm)
        else:
            q, k, v, r, ga = _gla_in(x, g_pre, w, tm)
            if sample is None:
                o, gla = _gla(q, k, v, ga, None, gla, i, 64, 16, _tile(t, 256), 1)
            else:
                t_new = t // sample['nseq']
                per_step = 8 if sample['nseq'] % 8 == 0 else 1
                o, gla = _gla(q, k, v, ga, sample['state_gla'], gla, i, t_new, t_new, per_step * t_new, per_step)
            x = _gla_out(x, o, r, w['g_out'], w['w_out'], g_post, tm)
        prev = None if sample is None else sample['state_ffn_conv'][l]
        x, c = _ffn(x, W['g_ffn_pre'][l][None], W['g_ffn_post'][l][None], prep['w_up'][l], W['w_ffn_conv'][l],
                    W['b_ffn_conv'][l][None], prep['w_down'][l], prev, _tile(t, 256))
        conv.append(c)
    return (x,) + tuple(caches) + (gla, jnp.stack(conv))


def kernel(x_prompt, x_sample, cache_mla_latent, cache_mla_rope, cache_diff_k, cache_diff_v, state_gla, state_ffn_conv, page_table, g_mix_pre, g_mix_post, g_ffn_pre, g_ffn_post, w_attn_in, g_q_norm, w_uq, g_kv_norm, w_uk, w_uv, lambda_q1, lambda_k1, lambda_q2, lambda_k2, g_diff_subln, w_attn_out, w_gla_in, w_gla_gate, b_gla_gate, g_gla_out, w_gla_out, w_ffn_up, w_ffn_conv, b_ffn_conv, w_ffn_down):
    W = dict(g_mix_pre=g_mix_pre, g_mix_post=g_mix_post, g_ffn_pre=g_ffn_pre, g_ffn_post=g_ffn_post,
             w_attn_in=w_attn_in, g_q_norm=g_q_norm, w_uq=w_uq, g_kv_norm=g_kv_norm, w_uk=w_uk, w_uv=w_uv,
             lambda_q1=lambda_q1, lambda_k1=lambda_k1, lambda_q2=lambda_q2, lambda_k2=lambda_k2,
             g_diff_subln=g_diff_subln, w_attn_out=w_attn_out, w_gla_in=w_gla_in, w_gla_gate=w_gla_gate,
             b_gla_gate=b_gla_gate, g_gla_out=g_gla_out, w_gla_out=w_gla_out, w_ffn_conv=w_ffn_conv,
             b_ffn_conv=b_ffn_conv)
    prep = {l: (_prep_attn(l // 2, W) if l % 2 == 0 else _prep_gla(l // 2, W)) for l in range(DEPTH)}
    prep['w_up'] = w_ffn_up.astype(BF16)
    prep['w_down'] = w_ffn_down.astype(BF16)

    bp, sp, d = x_prompt.shape
    bs, ts, _ = x_sample.shape
    n_pages = page_table.shape[1]
    past = n_pages * PAGE

    yp, p_lat, p_rope, p_k, p_v, p_gla, p_conv = _trunk(x_prompt, _rope_tables(jnp.arange(sp)), W, prep, None)

    la, n_pool = cache_mla_latent.shape[:2]
    sample = dict(
        lat=cache_mla_latent,
        ropeT=jnp.swapaxes(cache_mla_rope, 2, 3),
        dkT=jnp.transpose(cache_diff_k, (0, 1, 3, 4, 5, 2)),
        dv=cache_diff_v.reshape(la, n_pool, PAGE * DIFF_HEADS, DIFF_V),
        page_table=page_table, pps=next(p for p in (16, 8, 4, 2, 1) if n_pages % p == 0), nseq=bs,
        state_gla=state_gla, state_ffn_conv=state_ffn_conv)
    tab_s = jnp.tile(_rope_tables(past + jnp.arange(ts)), (bs, 1))
    ys, s_lat, s_rope, s_k, s_v, s_gla, s_conv = _trunk(x_sample.reshape(1, bs * ts, d), tab_s, W, prep, sample)

    def k6(kt, b, t):
        nb = kt.shape[1]
        kt = kt.reshape(kt.shape[0], nb, DIFF_HEADS, 2, DIFF_DH, b // nb, t)
        return jnp.transpose(kt, (0, 1, 5, 6, 2, 3, 4)).reshape(kt.shape[0], b, t, DIFF_HEADS, 2, DIFF_DH)

    def v5(a, b, t):
        return a.reshape(a.shape[0], b, t, DIFF_HEADS, DIFF_V)

    return (yp, ys.reshape(bs, ts, d),
            p_lat, p_rope, k6(p_k, bp, sp), v5(p_v, bp, sp), p_gla, p_conv,
            s_lat.reshape(-1, bs, ts, KV_LORA), s_rope.reshape(-1, bs, ts, MLA_ROPE), k6(s_k, bs, ts), v5(s_v, bs, ts),
            s_gla, s_conv)
```

```python
import functools
import math

import jax
import jax.numpy as jnp
from jax import lax
from jax.experimental import pallas as pl
from jax.experimental.pallas import tpu as pltpu

F32 = jnp.float32
BF16 = jnp.bfloat16

D_MODEL = 1024
DEPTH = 4
PAGE = 128
MLA_HEADS = 8
MLA_NOPE = 64
MLA_ROPE = 32
MLA_V = 64
Q_LORA = 256
KV_LORA = 128
DIFF_HEADS = 4
DIFF_DH = 64
DIFF_V = 128
GLA_HEADS = 4
GLA_DK = 128
GLA_DV = 256
GLA_RANK = 16
GLA_TAU = 16.0
D_FF = 2816
ROPE_THETA = 10000.0
EPS = 1e-6
NEG = -1e30
LOG2E = math.log2(math.e)
MLA_SCALE = (MLA_NOPE + MLA_ROPE) ** -0.5 * LOG2E
DIFF_SCALE = DIFF_DH ** -0.5 * LOG2E

LANES = 128
VMEM_LIMIT = 56 * 1024 * 1024

A_CQ, A_CKV, A_DQ, A_DK, A_DV, A_KR, A_W = 0, 256, 384, 896, 1408, 1920, 2048
G_Q, G_K, G_V, G_R, G_A, G_W = 0, 512, 1024, 2048, 3072, 3200

NT = (((1,), (1,)), ((), ()))
TN = (((0,), (0,)), ((), ()))


def _cp(*sem):
    return pltpu.CompilerParams(dimension_semantics=sem, vmem_limit_bytes=VMEM_LIMIT)


def _rms(x, g):
    return x * lax.rsqrt(jnp.mean(x * x, axis=-1, keepdims=True) + EPS) * g


def _dot(a, b):
    return jnp.dot(a, b, preferred_element_type=F32)


def _const_spec(shape):
    nd = len(shape)
    return pl.BlockSpec(shape, lambda *_: (0,) * nd)


def _rope_tables(pos):
    pos = pos.astype(F32)[:, None]
    lane = jnp.arange(LANES)

    def one(d):
        half = d // 2
        inv = ROPE_THETA ** (-jnp.arange(half, dtype=F32) * (2.0 / d))
        ang = pos * inv
        cos = jnp.tile(jnp.cos(ang), (1, LANES // half))
        sin = jnp.tile(jnp.sin(ang), (1, LANES // half))
        first = (lane % d) < half
        return [cos, jnp.where(first, -sin, 0.0), jnp.where(first, 0.0, sin)]

    return jnp.concatenate(one(DIFF_DH) + one(MLA_ROPE), axis=1)


def _rope(x, cos, sin_a, sin_b, half):
    return x * cos + pltpu.roll(x, LANES - half, 1) * sin_a + pltpu.roll(x, half, 1) * sin_b


def _attn_in_kernel(x_ref, g_ref, win_ref, gq_ref, wuq_ref, gkv_ref, wuk_ref, tab_ref, *rest):
    qm_ref, kcat_ref, dq_ref, dkb_ref, dvb_ref, ckv_ref, kr_ref, dkt_ref, dv4_ref = rest[-9:]
    tm = x_ref.shape[0]
    h = _rms(x_ref[...], g_ref[...]).astype(BF16)
    p = _dot(h, win_ref[...])
    tab = tab_ref[...]
    c64, a64, b64, c32, a32, b32 = [tab[:, i * LANES:(i + 1) * LANES] for i in range(6)]

    cqn = _rms(p[:, A_CQ:A_CKV], gq_ref[...]).astype(BF16)
    q = _dot(cqn, wuq_ref[...])
    nope_w = MLA_HEADS * MLA_NOPE
    qlat = _dot(q[:, :nope_w].astype(BF16), wuk_ref[...])
    for hh in range(MLA_HEADS):
        ql = qlat[:, hh * LANES:(hh + 1) * LANES]
        qr = _rope(q[:, nope_w + hh * LANES:nope_w + (hh + 1) * LANES], c32, a32, b32, MLA_ROPE // 2)
        qm_ref[hh] = (jnp.concatenate([ql, qr], axis=1) * MLA_SCALE).astype(qm_ref.dtype)

    ckv = _rms(p[:, A_CKV:A_DQ], gkv_ref[...])
    ckv_ref[...] = ckv
    krs = _rope(p[:, A_KR:A_W], c32, a32, b32, MLA_ROPE // 2)
    kr_ref[...] = krs[:, :MLA_ROPE]
    kcat_ref[...] = jnp.concatenate([ckv, krs], axis=1).astype(kcat_ref.dtype)
    for s in range(4):
        sl = slice(s * LANES, (s + 1) * LANES)
        dq = _rope(p[:, A_DQ + s * LANES:A_DQ + (s + 1) * LANES], c64, a64, b64, DIFF_DH // 2)
        dq_ref[:, sl] = (dq * DIFF_SCALE).astype(dq_ref.dtype)
        dk = _rope(p[:, A_DK + s * LANES:A_DK + (s + 1) * LANES], c64, a64, b64, DIFF_DH // 2)
        dkt_ref[sl, :] = dk.T
        dkb_ref[:, sl] = dk.astype(dkb_ref.dtype)
    dv = p[:, A_DV:A_KR]
    for hh in range(DIFF_HEADS):
        dv4_ref[pl.ds(hh, tm, stride=DIFF_HEADS), :] = dv[:, hh * LANES:(hh + 1) * LANES]
    dvb_ref[...] = dv.astype(dvb_ref.dtype)


def _attn_in(x, g, w, tab, act_dtype, tm, layer, bufs):
    nb, t, d = x.shape
    n_layers = bufs[0].shape[0]
    grid = (nb, t // tm)
    row = lambda c: pl.BlockSpec((None, tm, c), lambda b, i: (b, i, 0))
    lrow = lambda c: pl.BlockSpec((None, None, tm, c), lambda b, i: (layer, b, i, 0))
    out_shape = (
        jax.ShapeDtypeStruct((nb, MLA_HEADS, t, 2 * LANES), act_dtype),
        jax.ShapeDtypeStruct((nb, t, 2 * LANES), act_dtype),
        jax.ShapeDtypeStruct((nb, t, 512), act_dtype),
        jax.ShapeDtypeStruct((nb, t, 512), act_dtype),
        jax.ShapeDtypeStruct((nb, t, 512), act_dtype),
        jax.ShapeDtypeStruct((n_layers, nb, t, KV_LORA), F32),
        jax.ShapeDtypeStruct((n_layers, nb, t, MLA_ROPE), F32),
        jax.ShapeDtypeStruct((n_layers, nb, 512, t), F32),
        jax.ShapeDtypeStruct((n_layers, nb, t * DIFF_HEADS, DIFF_V), F32),
    )
    out_specs = (
        pl.BlockSpec((None, MLA_HEADS, tm, 2 * LANES), lambda b, i: (b, 0, i, 0)),
        row(2 * LANES), row(512), row(512), row(512), lrow(KV_LORA), lrow(MLA_ROPE),
        pl.BlockSpec((None, None, 512, tm), lambda b, i: (layer, b, 0, i)),
        pl.BlockSpec((None, None, tm * DIFF_HEADS, DIFF_V), lambda b, i: (layer, b, i, 0)),
    )
    in_specs = [
        row(d), _const_spec((1, d)), _const_spec(w['w_in'].shape), _const_spec((1, Q_LORA)),
        _const_spec(w['w_uq'].shape), _const_spec((1, KV_LORA)), _const_spec(w['w_ukbd'].shape),
        pl.BlockSpec((tm, 6 * LANES), lambda b, i: (i, 0)),
    ]
    args = [x, g, w['w_in'], w['g_q'], w['w_uq'], w['g_kv'], w['w_ukbd'], tab]
    aliases = {}
    for j, buf in enumerate(bufs):
        assert buf.shape == out_shape[5 + j].shape
        aliases[len(args)] = 5 + j
        in_specs.append(pl.BlockSpec(memory_space=pl.ANY))
        args.append(buf)
    return pl.pallas_call(
        _attn_in_kernel, grid=grid, in_specs=in_specs, out_specs=out_specs, out_shape=out_shape,
        input_output_aliases=aliases, compiler_params=_cp("parallel", "parallel"), name="attn_in",
    )(*args)


def _online_update(s, v, m_ref, l_ref, acc_ref):
    m_prev = m_ref[...]
    m_new = jnp.maximum(m_prev, jnp.max(s, axis=1, keepdims=True))
    alpha = jnp.exp2(m_prev - m_new)
    p = jnp.exp2(s - m_new)
    l_ref[...] = alpha * l_ref[...] + jnp.sum(p, axis=1, keepdims=True)
    acc_ref[...] = alpha * acc_ref[...] + _dot(p.astype(v.dtype), v)
    m_ref[...] = m_new


def _lambda(lam_ref, lam_init):
    lv = lam_ref[...]
    s1 = jnp.sum(lv[0:1] * lv[1:2], axis=1, keepdims=True)
    s2 = jnp.sum(lv[2:3] * lv[3:4], axis=1, keepdims=True)
    return jnp.exp(s1) - jnp.exp(s2) + lam_init


def _split_maps(dqh):
    lane = lax.broadcasted_iota(jnp.int32, dqh.shape, 1)
    zero = jnp.zeros_like(dqh)
    return jnp.concatenate([jnp.where(lane < DIFF_DH, dqh, zero), jnp.where(lane >= DIFF_DH, dqh, zero)], axis=0)


FLASH_TILES = MLA_HEADS + 2 * DIFF_HEADS
FLASH_KCHUNK = 512
FLASH_AHEAD = 2


def _flash_kernel(qm_ref, kcat_ref, dq_ref, dk_ref, dv_ref, lam_ref, olat_ref, odiff_ref,
                  dqs, m_ref, l_ref, acc_ref, *, tq, lam_init):
    qi = pl.program_id(1)
    m_ref[...] = jnp.full(m_ref.shape, NEG, F32)
    l_ref[...] = jnp.zeros(l_ref.shape, F32)
    acc_ref[...] = jnp.zeros(acc_ref.shape, F32)
    lane = lax.broadcasted_iota(jnp.int32, (tq, LANES), 1)
    for h in range(DIFF_HEADS):
        dqh = dq_ref[:, h * LANES:(h + 1) * LANES]
        zero = jnp.zeros_like(dqh)
        dqs[2 * h] = jnp.where(lane < DIFF_DH, dqh, zero)
        dqs[2 * h + 1] = jnp.where(lane >= DIFF_DH, dqh, zero)

    def chunk(k0, w, masked):
        kc = kcat_ref[pl.ds(k0, w), :]
        nl = w // LANES

        def scores(t):
            if t < MLA_HEADS:
                q, kk = qm_ref[t], kc
            else:
                h = (t - MLA_HEADS) // 2
                q = dqs[t - MLA_HEADS]
                kk = dk_ref[pl.ds(k0, w), h * LANES:(h + 1) * LANES]
            s = lax.dot_general(q, kk, NT, preferred_element_type=F32)
            if masked:
                row = lax.broadcasted_iota(jnp.int32, s.shape, 0)
                col = lax.broadcasted_iota(jnp.int32, s.shape, 1)
                s = jnp.where(col <= row, s, NEG)
            return s

        def update(t, s):
            if t < MLA_HEADS:
                v = kc[:, :KV_LORA]
            else:
                h = (t - MLA_HEADS) // 2
                v = dv_ref[pl.ds(k0, w), h * LANES:(h + 1) * LANES]
            mx = s[:, :LANES]
            for j in range(1, nl):
                mx = jnp.maximum(mx, s[:, j * LANES:(j + 1) * LANES])
            m_prev = m_ref[t]
            m_new = jnp.maximum(m_prev, jnp.max(mx, axis=1, keepdims=True))
            alpha = jnp.exp2(m_prev - m_new)
            ps = None
            pbs = []
            for j in range(nl):
                p = jnp.exp2(s[:, j * LANES:(j + 1) * LANES] - m_new)
                ps = p if ps is None else ps + p
                pbs.append(p.astype(BF16))
            l_ref[t] = alpha * l_ref[t] + ps
            acc_ref[t] = alpha * acc_ref[t] + _dot(jnp.concatenate(pbs, axis=1), v)
            m_ref[t] = m_new

        pend = [scores(t) for t in range(FLASH_AHEAD)]
        for t in range(FLASH_TILES):
            if t + FLASH_AHEAD < FLASH_TILES:
                pend.append(scores(t + FLASH_AHEAD))
            update(t, pend.pop(0))

    def body(j, carry):
        chunk(pl.multiple_of(j * FLASH_KCHUNK, FLASH_KCHUNK), FLASH_KCHUNK, False)
        return carry

    lax.fori_loop(0, (qi * tq) // FLASH_KCHUNK, body, 0)
    per = FLASH_KCHUNK // tq
    for r in range(1, per):
        @pl.when(qi % per >= r)
        def _():
            chunk(pl.multiple_of((qi - qi % per + (r - 1)) * tq, tq), tq, False)
    chunk(pl.multiple_of(qi * tq, tq), tq, True)

    lam = _lambda(lam_ref, lam_init)
    outs = [acc_ref[t] * (1.0 / jnp.sum(l_ref[t], axis=1, keepdims=True)) for t in range(FLASH_TILES)]
    for h in range(MLA_HEADS):
        olat_ref[:, h * LANES:(h + 1) * LANES] = outs[h].astype(olat_ref.dtype)
    for h in range(DIFF_HEADS):
        odiff_ref[:, h * LANES:(h + 1) * LANES] = outs[MLA_HEADS + 2 * h] - lam * outs[MLA_HEADS + 2 * h + 1]


def _flash(qm, kcat, dq, dk, dv, lam, lam_init, tq):
    nb, _, s, _ = qm.shape
    assert FLASH_KCHUNK % tq == 0 and s % tq == 0
    whole = lambda w: pl.BlockSpec((None, s, w), lambda b, qi: (b, 0, 0))
    in_specs = [
        pl.BlockSpec((None, MLA_HEADS, tq, 2 * LANES), lambda b, qi: (b, 0, qi, 0)),
        whole(2 * LANES),
        pl.BlockSpec((None, tq, 512), lambda b, qi: (b, qi, 0)),
        whole(512), whole(512), _const_spec((8, LANES)),
    ]
    out_specs = (pl.BlockSpec((None, tq, MLA_HEADS * LANES), lambda b, qi: (b, qi, 0)),
                 pl.BlockSpec((None, tq, 512), lambda b, qi: (b, qi, 0)))
    out_shape = (jax.ShapeDtypeStruct((nb, s, MLA_HEADS * LANES), BF16), jax.ShapeDtypeStruct((nb, s, 512), F32))
    scratch = [pltpu.VMEM((2 * DIFF_HEADS, tq, LANES), BF16), pltpu.VMEM((FLASH_TILES, tq, LANES), F32),
               pltpu.VMEM((FLASH_TILES, tq, LANES), F32), pltpu.VMEM((FLASH_TILES, tq, LANES), F32)]
    return pl.pallas_call(
        functools.partial(_flash_kernel, tq=tq, lam_init=lam_init), grid=(nb, s // tq), in_specs=in_specs,
        out_specs=out_specs, out_shape=out_shape, scratch_shapes=scratch,
        compiler_params=_cp("parallel", "arbitrary"), name="flash_prompt",
    )(qm, kcat, dq, dk, dv, lam)


DECODE_GROUPS = 2


def _decode_kernel(pt_ref, qm_ref, dq_ref, kcat_ref, dkn_ref, dvn_ref, lam_ref, *rest, pps, t_new, lam_init):
    lat_refs = rest[0 * pps:1 * pps]
    rope_refs = rest[1 * pps:2 * pps]
    dk_refs = rest[2 * pps:3 * pps]
    dv_refs = rest[3 * pps:4 * pps]
    olat_ref, odiff_ref, qs, qbd, m1, l1, acc1, m2, l2, acc2 = rest[4 * pps:]
    c = pl.program_id(1)
    nc = pl.num_programs(1)
    rows = MLA_HEADS * t_new
    drows = 2 * t_new

    @pl.when(c == 0)
    def _():
        qs[...] = qm_ref[...].reshape(rows, 2 * LANES)
        dq = dq_ref[...]
        zeros = jnp.zeros((drows, LANES), F32)
        for h in range(DIFF_HEADS):
            blk = _split_maps(dq[:, h * LANES:(h + 1) * LANES])
            qbd[h * drows:(h + 1) * drows, :] = jnp.concatenate(
                [blk if g == h else zeros for g in range(DIFF_HEADS)], axis=1)
        m1[...] = jnp.full(m1.shape, NEG, F32)
        l1[...] = jnp.zeros(l1.shape, F32)
        acc1[...] = jnp.zeros(acc1.shape, F32)
        m2[...] = jnp.full(m2.shape, NEG, F32)
        l2[...] = jnp.zeros(l2.shape, F32)
        acc2[...] = jnp.zeros(acc2.shape, F32)

    q = qs[...]
    ql = q[:, :KV_LORA].astype(BF16)
    qr = q[:, KV_LORA:KV_LORA + MLA_ROPE].astype(BF16)
    qd = qbd[...]
    qdb = qd.astype(BF16)

    def diag_blocks(wide):
        return jnp.concatenate([wide[h * drows:(h + 1) * drows, h * LANES:(h + 1) * LANES]
                                for h in range(DIFF_HEADS)], axis=0)

    def softmax_pv(s, pv_of, m_ref, l_ref, acc_ref):
        m_prev = m_ref[...]
        m_new = jnp.maximum(m_prev, jnp.max(s, axis=1, keepdims=True))
        alpha = jnp.exp2(m_prev - m_new)
        p = jnp.exp2(s - m_new)
        l_ref[...] = alpha * l_ref[...] + jnp.sum(p, axis=1, keepdims=True)
        acc_ref[...] = alpha * acc_ref[...] + pv_of(p)
        m_ref[...] = m_new

    def paged_pv(values, pages):
        def pv_of(p):
            pb = p.astype(BF16)
            pv = None
            for n, j in enumerate(pages):
                t = _dot(pb[:, n * PAGE:(n + 1) * PAGE], values(j))
                pv = t if pv is None else pv + t
            return pv
        return pv_of

    lat = [r[...].astype(BF16) for r in lat_refs]
    kt_rows = DIFF_HEADS * 2 * DIFF_DH

    def mla_scores(pages):
        return jnp.concatenate(
            [lax.dot_general(ql, lat[j], NT, preferred_element_type=F32) + _dot(qr, rope_refs[j][...].astype(BF16))
             for j in pages], axis=1)

    def diff_scores(pages):
        return jnp.concatenate(
            [_dot(qdb, dk_refs[j][...].reshape(kt_rows, PAGE).astype(BF16)) for j in pages], axis=1)

    def v_wide(j):
        return jnp.concatenate([dv_refs[j][pl.ds(h, PAGE, stride=DIFF_HEADS), :] for h in range(DIFF_HEADS)],
                               axis=1).astype(BF16)

    ngroups = DECODE_GROUPS if pps % DECODE_GROUPS == 0 else 1
    groups = [list(range(g * pps // ngroups, (g + 1) * pps // ngroups)) for g in range(ngroups)]
    pend = [(mla_scores(groups[0]), diff_scores(groups[0]))]
    for g, pages in enumerate(groups):
        if g + 1 < ngroups:
            pend.append((mla_scores(groups[g + 1]), diff_scores(groups[g + 1])))
        s_mla, s_diff = pend.pop(0)
        softmax_pv(s_mla, paged_pv(lambda j: lat[j], pages), m1, l1, acc1)
        softmax_pv(s_diff, lambda p: diag_blocks(paged_pv(v_wide, pages)(p)), m2, l2, acc2)

    @pl.when(c == nc - 1)
    def _():
        def causal(s):
            row = lax.broadcasted_iota(jnp.int32, s.shape, 0) & (t_new - 1)
            col = lax.broadcasted_iota(jnp.int32, s.shape, 1)
            return jnp.where(col <= row, s, NEG)

        kc = kcat_ref[...]
        s = causal(lax.dot_general(q, kc, NT, preferred_element_type=F32))
        softmax_pv(s, lambda p: _dot(p, kc[:, :KV_LORA]), m1, l1, acc1)
        o = acc1[...] / l1[...]
        for h in range(MLA_HEADS):
            olat_ref[:, h * LANES:(h + 1) * LANES] = o[h * t_new:(h + 1) * t_new].astype(olat_ref.dtype)
        s = causal(lax.dot_general(qd, dkn_ref[...], NT, preferred_element_type=F32))
        softmax_pv(s, lambda p: diag_blocks(_dot(p, dvn_ref[...])), m2, l2, acc2)
        o = acc2[...] / l2[...]
        lam = _lambda(lam_ref, lam_init)
        for h in range(DIFF_HEADS):
            oh = o[h * drows:(h + 1) * drows]
            odiff_ref[:, h * LANES:(h + 1) * LANES] = oh[:t_new] - lam * oh[t_new:]


def _decode(layer, qm, dq, kcat, dkn, dvn, lam, lam_init, lat_pool, ropeT_pool, dkT_pool, dv_pool, page_table, pps):
    _, _, tot, _ = qm.shape
    nbatch, n_pages = page_table.shape
    t_new = tot // nbatch
    nc = n_pages // pps
    rows = MLA_HEADS * t_new

    def page(j):
        return lambda b, c, pt: (layer, pt[b, c * pps + j]) + (0,) * 2

    def page4(j):
        return lambda b, c, pt: (layer, pt[b, c * pps + j], 0, 0, 0, 0)

    new = lambda w: pl.BlockSpec((None, t_new, w), lambda b, c, pt: (0, b, 0))
    in_specs = [
        pl.BlockSpec((None, MLA_HEADS, t_new, 2 * LANES), lambda b, c, pt: (0, 0, b, 0)),
        new(512), new(2 * LANES), new(512), new(512),
        pl.BlockSpec((8, LANES), lambda b, c, pt: (0, 0)),
    ]
    in_specs += [pl.BlockSpec((None, None, PAGE, KV_LORA), page(j)) for j in range(pps)]
    in_specs += [pl.BlockSpec((None, None, MLA_ROPE, PAGE), page(j)) for j in range(pps)]
    in_specs += [pl.BlockSpec((None, None, DIFF_HEADS, 2, DIFF_DH, PAGE), page4(j)) for j in range(pps)]
    in_specs += [pl.BlockSpec((None, None, PAGE * DIFF_HEADS, DIFF_V), page(j)) for j in range(pps)]
    out_specs = (new(MLA_HEADS * LANES), new(512))
    out_shape = (jax.ShapeDtypeStruct((1, tot, MLA_HEADS * LANES), F32),
                 jax.ShapeDtypeStruct((1, tot, 512), F32))
    drows = DIFF_HEADS * 2 * t_new
    scratch = [
        pltpu.VMEM((rows, 2 * LANES), F32), pltpu.VMEM((drows, DIFF_HEADS * LANES), F32),
        pltpu.VMEM((rows, 1), F32), pltpu.VMEM((rows, 1), F32), pltpu.VMEM((rows, KV_LORA), F32),
        pltpu.VMEM((drows, 1), F32), pltpu.VMEM((drows, 1), F32), pltpu.VMEM((drows, DIFF_V), F32),
    ]
    grid_spec = pltpu.PrefetchScalarGridSpec(
        num_scalar_prefetch=1, grid=(nbatch, nc), in_specs=in_specs, out_specs=out_specs, scratch_shapes=scratch)
    return pl.pallas_call(
        functools.partial(_decode_kernel, pps=pps, t_new=t_new, lam_init=lam_init),
        grid_spec=grid_spec, out_shape=out_shape,
        compiler_params=_cp("parallel", "arbitrary"), name="decode_attn",
    )(page_table, qm, dq, kcat, dkn, dvn, lam,
      *([lat_pool] * pps), *([ropeT_pool] * pps), *([dkT_pool] * pps), *([dv_pool] * pps))


def _attn_mix(x, olat_ref, odiff_ref, wuv_ref, gsub_ref, wo_ref, gpost_ref, scale):
    o_mla = _dot(olat_ref[...].astype(BF16), wuv_ref[...])
    od = odiff_ref[...]
    parts = [o_mla]
    for h in range(DIFF_HEADS):
        parts.append(_rms(od[:, h * LANES:(h + 1) * LANES], gsub_ref[...]) * scale)
    mix = _dot(jnp.concatenate(parts, axis=1).astype(BF16), wo_ref[...])
    return x + _rms(mix, gpost_ref[...])


def _gla_mix(x, o_ref, r_ref, gout_ref, wo_ref, gpost_ref):
    o = o_ref[...]
    parts = [_rms(o[:, h * GLA_DV:(h + 1) * GLA_DV], gout_ref[...]) for h in range(GLA_HEADS)]
    y = jnp.concatenate(parts, axis=1) * jax.nn.silu(r_ref[...])
    mix = _dot(y.astype(BF16), wo_ref[...])
    return x + _rms(mix, gpost_ref[...])


FFN_CHUNK = 256
N_MIX_REFS = {'attn': 6, 'gla': 5}


def _ffn_kernel(x_ref, *refs, period, has_prev, mixer, scale):
    mix_refs, refs = refs[:N_MIX_REFS[mixer]], refs[N_MIX_REFS[mixer]:]
    if has_prev:
        gpre_ref, gpost_ref, wup_ref, wconv_ref, bconv_ref, wdown_ref, prev_ref, o_ref, st_ref, carry = refs
    else:
        gpre_ref, gpost_ref, wup_ref, wconv_ref, bconv_ref, wdown_ref, o_ref, st_ref, carry = refs
    tm = x_ref.shape[0]
    i = pl.program_id(1)
    if mixer == 'attn':
        x = _attn_mix(x_ref[...], *mix_refs, scale)
    else:
        x = _gla_mix(x_ref[...], *mix_refs)
    h = _rms(x, gpre_ref[...]).astype(BF16)
    row = lax.broadcasted_iota(jnp.int32, (tm, FFN_CHUNK), 0)
    t = row & (period - 1) if period < tm else row

    if not has_prev:
        @pl.when(i == 0)
        def _():
            carry[...] = jnp.zeros(carry.shape, F32)

    def up(c):
        return (_dot(h, wup_ref[:, c * FFN_CHUNK:(c + 1) * FFN_CHUNK]),
                _dot(h, wup_ref[:, D_FF + c * FFN_CHUNK:D_FF + (c + 1) * FFN_CHUNK]))

    nchunk = D_FF // FFN_CHUNK
    y = None
    nxt = up(0)
    for c in range(nchunk):
        sl = slice(c * FFN_CHUNK, (c + 1) * FFN_CHUNK)
        a, v = nxt
        if c + 1 < nchunk:
            nxt = up(c + 1)
        if has_prev:
            nseq = tm // period
            pr = prev_ref[:, :, sl]
            p0 = jnp.broadcast_to(pr[:, 0:1, :], (nseq, period, FFN_CHUNK)).reshape(tm, FFN_CHUNK)
            p1 = jnp.broadcast_to(pr[:, 1:2, :], (nseq, period, FFN_CHUNK)).reshape(tm, FFN_CHUNK)
            st_ref[:, :, sl] = a.reshape(nseq, period, FFN_CHUNK)[:, period - 2:, :]
        else:
            cr = carry[:, sl]
            p0 = cr[6:7, :]
            p1 = cr[7:8, :]
            carry[:, sl] = a[tm - 8:, :]
            st_ref[:, sl] = a[tm - 2:, :]
        am1 = jnp.where(t == 0, p1, pltpu.roll(a, 1, 0))
        am2 = jnp.where(t == 0, p0, jnp.where(t == 1, p1, pltpu.roll(a, 2, 0)))
        wc = wconv_ref[:, sl]
        ac = bconv_ref[:, sl] + wc[0:1] * am2 + wc[1:2] * am1 + wc[2:3] * a
        gelu = 0.5 * ac * (1.0 + lax.erf(ac * math.sqrt(0.5)))
        z = (gelu * v).astype(BF16)
        yc = _dot(z, wdown_ref[sl, :])
        y = yc if y is None else y + yc
    o_ref[...] = x + _rms(y, gpost_ref[...])


def _ffn(x, mixer, mix_rows, mix_consts, scale, gpre, gpost, w_up, w_conv, b_conv, w_down, prev, tm):
    nb, t, d = x.shape
    has_prev = prev is not None
    row = pl.BlockSpec((None, tm, d), lambda b, i: (b, i, 0))
    rowc = lambda c: pl.BlockSpec((None, tm, c), lambda b, i: (b, i, 0))
    in_specs = [row] + [rowc(a.shape[-1]) for a in mix_rows] + [_const_spec(a.shape) for a in mix_consts]
    assert len(mix_rows) + len(mix_consts) == N_MIX_REFS[mixer]
    in_specs += [_const_spec((1, d)), _const_spec((1, d)), _const_spec(w_up.shape), _const_spec((3, D_FF)),
                 _const_spec((1, D_FF)), _const_spec(w_down.shape)]
    args = [x, *mix_rows, *mix_consts, gpre, gpost, w_up, w_conv, b_conv, w_down]
    if has_prev:
        nseq = prev.shape[0]
        period = t // nseq
        spt = tm // period
        in_specs.append(pl.BlockSpec((spt, 2, D_FF), lambda b, i: (i, 0, 0)))
        args.append(prev)
        st_spec = pl.BlockSpec((spt, 2, D_FF), lambda b, i: (i, 0, 0))
        st_shape = jax.ShapeDtypeStruct((nseq, 2, D_FF), F32)
    else:
        period = t
        st_spec = pl.BlockSpec((None, 2, D_FF), lambda b, i: (b, 0, 0))
        st_shape = jax.ShapeDtypeStruct((nb, 2, D_FF), F32)
    return pl.pallas_call(
        functools.partial(_ffn_kernel, period=period, has_prev=has_prev, mixer=mixer, scale=scale),
        grid=(nb, t // tm), in_specs=in_specs,
        out_specs=(row, st_spec), out_shape=(jax.ShapeDtypeStruct(x.shape, F32), st_shape),
        scratch_shapes=[pltpu.VMEM((8, D_FF), F32)],
        compiler_params=_cp("parallel", "arbitrary"), name="conv_ffn",
    )(*args)


def _gla_in_kernel(x_ref, g_ref, win_ref, wg_ref, bg_ref, q_ref, k_ref, v_ref, r_ref, ga_ref):
    h = _rms(x_ref[...], g_ref[...]).astype(BF16)
    p = _dot(h, win_ref[...])
    q_ref[...] = p[:, G_Q:G_K] * (GLA_DK ** -0.5)
    k_ref[...] = p[:, G_K:G_V]
    v_ref[...] = p[:, G_V:G_R].astype(v_ref.dtype)
    r_ref[...] = p[:, G_R:G_A]
    z = _dot(p[:, G_A:G_W].astype(BF16), wg_ref[...]) + bg_ref[...]
    ga_ref[...] = (jnp.minimum(z, 0.0) - jnp.log1p(jnp.exp(-jnp.abs(z)))) * (LOG2E / GLA_TAU)


def _gla_in(x, g, w, tm):
    nb, t, d = x.shape
    row = lambda c: pl.BlockSpec((None, tm, c), lambda b, i: (b, i, 0))
    sds = lambda c, dt: jax.ShapeDtypeStruct((nb, t, c), dt)
    return pl.pallas_call(
        _gla_in_kernel, grid=(nb, t // tm),
        in_specs=[row(d), _const_spec((1, d)), _const_spec(w['w_in'].shape), _const_spec(w['w_gate'].shape),
                  _const_spec((1, 512))],
        out_specs=(row(512), row(512), row(1024), row(1024), row(512)),
        out_shape=(sds(512, F32), sds(512, F32), sds(1024, BF16), sds(1024, F32), sds(512, F32)),
        compiler_params=_cp("parallel", "parallel"), name="gla_in",
    )(x, g, w['w_in'], w['w_gate'], w['b_gate'])


def _gla_chunk(q_ref, k_ref, v_ref, g_ref, b_ref, base, head, c, state, sub):
    mm = BF16 if c >= 16 else F32
    kcol = slice(head * GLA_DK, (head + 1) * GLA_DK)
    q = q_ref[base:base + c, kcol]
    k = k_ref[base:base + c, kcol]
    v = v_ref[base:base + c, head * GLA_DV:(head + 1) * GLA_DV].astype(mm)
    row = lax.broadcasted_iota(jnp.int32, (c, GLA_DK), 0)
    b = g_ref[base:base + c, kcol]
    step = 1
    while step < c:
        b = b + jnp.where(row >= step, pltpu.roll(b, step, 0), 0.0)
        step *= 2
    b_ref[...] = b
    b_end = b_ref[c - 1:c, :]
    lane_c = lax.broadcasted_iota(jnp.int32, (sub, c), 1)
    rsub = lax.broadcasted_iota(jnp.int32, (sub, c), 0)
    blocks = []
    for i in range(c // sub):
        r0 = i * sub
        qi = q[r0:r0 + sub]
        bi = b[r0:r0 + sub]
        d_i = jnp.zeros((sub, c), F32)
        for s in range(sub):
            r = r0 + s
            ks = k_ref[base + r:base + r + 1, kcol]
            bs = b_ref[r:r + 1, :]
            w = qi * ks * jnp.exp2(jnp.minimum(bi - bs, 0.0))
            col = jnp.sum(w, axis=1, keepdims=True)
            d_i = d_i + col * (lane_c[0:1] == r).astype(F32)
        a_i = jnp.where(lane_c - r0 <= rsub, d_i, 0.0)
        if i > 0:
            beta = b_ref[r0 - 1:r0, :]
            qt = (qi * jnp.exp2(bi - beta)).astype(mm)
            kt = (k * jnp.exp2(jnp.minimum(beta - b, 0.0))).astype(mm)
            below = lax.dot_general(qt, kt, NT, preferred_element_type=F32)
            a_i = jnp.where(lane_c < r0, below, a_i)
        blocks.append(a_i)
    a = blocks[0] if len(blocks) == 1 else jnp.concatenate(blocks, axis=0)
    qd = (q * jnp.exp2(b)).astype(mm)
    o = _dot(qd, state.astype(mm)) + _dot(a.astype(mm), v)
    kd = (k * jnp.exp2(b_end - b)).astype(mm)
    rr = lax.broadcasted_iota(jnp.int32, (GLA_DK, GLA_DK), 0)
    cc = lax.broadcasted_iota(jnp.int32, (GLA_DK, GLA_DK), 1)
    decay = jnp.sum(jnp.where(rr == cc, jnp.exp2(b_end), 0.0), axis=1, keepdims=True)
    new_state = decay * state + lax.dot_general(kd, v, TN, preferred_element_type=F32)
    return o, new_state


def _gla_kernel(*refs, chunk, sub, nseq, has_s0):
    if has_s0:
        q_ref, k_ref, v_ref, g_ref, s0_ref, _, o_ref, s_ref, st, b_scr = refs
    else:
        q_ref, k_ref, v_ref, g_ref, _, o_ref, s_ref, st, b_scr = refs

        @pl.when(pl.program_id(1) == 0)
        def _():
            st[...] = jnp.zeros(st.shape, F32)

    rows = q_ref.shape[0] // nseq
    for j in range(nseq):
        states = [s0_ref[j, h] if has_s0 else st[h] for h in range(GLA_HEADS)]
        for n in range(rows // chunk):
            base = j * rows + n * chunk
            for h in range(GLA_HEADS):
                o, states[h] = _gla_chunk(q_ref, k_ref, v_ref, g_ref, b_scr.at[h], base, h, chunk, states[h], sub)
                o_ref[base:base + chunk, h * GLA_DV:(h + 1) * GLA_DV] = o
        for h in range(GLA_HEADS):
            if not has_s0:
                st[h] = states[h]
            s_ref[j, h] = states[h]


def _gla(q, k, v, g, s0, s_buf, layer, chunk, sub, tt, nseq):
    nb, t, _ = q.shape
    has_s0 = s0 is not None
    assert (nb == 1) if has_s0 else (nseq == 1)
    blk = lambda w: pl.BlockSpec((None, tt, w), lambda b, i: (b, i, 0))
    st_blk = (None, nseq, GLA_HEADS, GLA_DK, GLA_DV)
    in_specs = [blk(GLA_HEADS * GLA_DK), blk(GLA_HEADS * GLA_DK), blk(GLA_HEADS * GLA_DV), blk(GLA_HEADS * GLA_DK)]
    args = [q, k, v, g]
    if has_s0:
        in_specs.append(pl.BlockSpec(st_blk, lambda b, i: (layer, i, 0, 0, 0)))
        args.append(s0)
        s_spec = pl.BlockSpec(st_blk, lambda b, i: (layer, i, 0, 0, 0))
        assert s_buf.shape[1] == (t // tt) * nseq
    else:
        s_spec = pl.BlockSpec(st_blk, lambda b, i: (layer, b, 0, 0, 0))
        assert s_buf.shape[1] == nb
    in_specs.append(pl.BlockSpec(memory_space=pl.ANY))
    args.append(s_buf)
    return pl.pallas_call(
        functools.partial(_gla_kernel, chunk=chunk, sub=sub, nseq=nseq, has_s0=has_s0),
        grid=(nb, t // tt), in_specs=in_specs, out_specs=(blk(GLA_HEADS * GLA_DV), s_spec),
        out_shape=(jax.ShapeDtypeStruct((nb, t, GLA_HEADS * GLA_DV), F32),
                   jax.ShapeDtypeStruct(s_buf.shape, F32)),
        input_output_aliases={len(args) - 1: 1},
        scratch_shapes=[pltpu.VMEM((GLA_HEADS, GLA_DK, GLA_DV), F32), pltpu.VMEM((GLA_HEADS, chunk, GLA_DK), F32)],
        compiler_params=_cp("parallel", "arbitrary"), name="gla_core",
    )(*args)


def _prep_attn(i, W):
    w_in = W['w_attn_in'][i]
    cq, ckv, kr, dq, dk, dv = jnp.split(w_in, [256, 384, 416, 928, 1440], axis=1)
    w_in2 = jnp.concatenate([cq, ckv, dq, dk, dv, kr, jnp.zeros((D_MODEL, A_W - A_KR - MLA_ROPE), F32)], axis=1)
    wq = W['w_uq'][i].reshape(Q_LORA, MLA_HEADS, MLA_NOPE + MLA_ROPE)
    w_nope = wq[:, :, :MLA_NOPE].reshape(Q_LORA, MLA_HEADS * MLA_NOPE)
    w_rope = jnp.pad(wq[:, :, MLA_NOPE:], ((0, 0), (0, 0), (0, LANES - MLA_ROPE))).reshape(Q_LORA, MLA_HEADS * LANES)
    eye = jnp.eye(MLA_HEADS, dtype=F32)
    w_ukbd = jnp.einsum('chn,hg->hngc', W['w_uk'][i], eye).reshape(MLA_HEADS * MLA_NOPE, MLA_HEADS * KV_LORA)
    w_uvbd = jnp.einsum('chv,hg->hcgv', W['w_uv'][i], eye).reshape(MLA_HEADS * KV_LORA, MLA_HEADS * MLA_V)
    lam = jnp.zeros((8, LANES), F32)
    for r, name in enumerate(('lambda_q1', 'lambda_k1', 'lambda_q2', 'lambda_k2')):
        lam = lam.at[r, :DIFF_DH].set(W[name][i])
    return dict(
        w_in=w_in2.astype(BF16), g_q=W['g_q_norm'][i][None], w_uq=jnp.concatenate([w_nope, w_rope], 1).astype(BF16),
        g_kv=W['g_kv_norm'][i][None], w_ukbd=w_ukbd.astype(BF16), w_uvbd=w_uvbd.astype(BF16),
        g_sub=W['g_diff_subln'][i][None], w_out=W['w_attn_out'][i].astype(BF16), lam=lam)


def _prep_gla(i, W):
    w_in = W['w_gla_in'][i]
    q, k, v, a, r = jnp.split(w_in, [512, 1024, 2048, 2064], axis=1)
    w_in2 = jnp.concatenate([q, k, v, r, a, jnp.zeros((D_MODEL, G_W - G_A - GLA_RANK), F32)], axis=1)
    w_gate = jnp.pad(W['w_gla_gate'][i], ((0, LANES - GLA_RANK), (0, 0)))
    return dict(w_in=w_in2.astype(BF16), w_gate=w_gate.astype(BF16), b_gate=W['b_gla_gate'][i][None],
                g_out=W['g_gla_out'][i][None], w_out=W['w_gla_out'][i].astype(BF16))


def _tile(t, pref):
    return pref if t % pref == 0 else t


def _trunk(x, tab, W, prep, sample):
    nb, t, d = x.shape
    tm = _tile(t, 512)
    act = F32 if sample is not None else BF16
    n_attn, n_gla = (DEPTH + 1) // 2, DEPTH // 2
    caches = (jnp.zeros((n_attn, nb, t, KV_LORA), F32), jnp.zeros((n_attn, nb, t, MLA_ROPE), F32),
              jnp.zeros((n_attn, nb, 512, t), F32), jnp.zeros((n_attn, nb, t * DIFF_HEADS, DIFF_V), F32))
    gla = jnp.zeros((n_gla, nb if sample is None else sample['nseq'], GLA_HEADS, GLA_DK, GLA_DV), F32)
    conv = []
    for l in range(DEPTH):
        i = l // 2
        g_pre = W['g_mix_pre'][l][None]
        g_post = W['g_mix_post'][l][None]
        w = prep[l]
        if l % 2 == 0:
            lam_init = 0.8 - 0.6 * math.exp(-0.3 * l)
            qm, kcat, dq, dkb, dvb, *caches = _attn_in(x, g_pre, w, tab, act, tm, i, caches)
            if sample is None:
                olat, odiff = _flash(qm, kcat, dq, dkb, dvb, w['lam'], lam_init, _tile(t, 256))
            else:
                olat, odiff = _decode(i, qm, dq, kcat, dkb, dvb, w['lam'], lam_init, sample['lat'], sample['ropeT'],
                                      sample['dkT'], sample['dv'], sample['page_table'], sample['pps'])
            mix = ('attn', (olat, odiff), (w['w_uvbd'], w['g_sub'], w['w_out'], g_post), 1.0 - lam_init)
        else:
            q, k, v, r, ga = _gla_in(x, g_pre, w, tm)
            if sample is None:
                o, gla = _gla(q, k, v, ga, None, gla, i, 64, 16, _tile(t, 256), 1)
            else:
                t_new = t // sample['nseq']
                per_step = 8 if sample['nseq'] % 8 == 0 else 1
                o, gla = _gla(q, k, v, ga, sample['state_gla'], gla, i, t_new, t_new, per_step * t_new, per_step)
            mix = ('gla', (o, r), (w['g_out'], w['w_out'], g_post), None)
        prev = None if sample is None else sample['state_ffn_conv'][l]
        x, c = _ffn(x, *mix, W['g_ffn_pre'][l][None], W['g_ffn_post'][l][None], prep['w_up'][l], W['w_ffn_conv'][l],
                    W['b_ffn_conv'][l][None], prep['w_down'][l], prev, _tile(t, 256))
        conv.append(c)
    return (x,) + tuple(caches) + (gla, jnp.stack(conv))


def kernel(x_prompt, x_sample, cache_mla_latent, cache_mla_rope, cache_diff_k, cache_diff_v, state_gla, state_ffn_conv, page_table, g_mix_pre, g_mix_post, g_ffn_pre, g_ffn_post, w_attn_in, g_q_norm, w_uq, g_kv_norm, w_uk, w_uv, lambda_q1, lambda_k1, lambda_q2, lambda_k2, g_diff_subln, w_attn_out, w_gla_in, w_gla_gate, b_gla_gate, g_gla_out, w_gla_out, w_ffn_up, w_ffn_conv, b_ffn_conv, w_ffn_down):
    W = dict(g_mix_pre=g_mix_pre, g_mix_post=g_mix_post, g_ffn_pre=g_ffn_pre, g_ffn_post=g_ffn_post,
             w_attn_in=w_attn_in, g_q_norm=g_q_norm, w_uq=w_uq, g_kv_norm=g_kv_norm, w_uk=w_uk, w_uv=w_uv,
             lambda_q1=lambda_q1, lambda_k1=lambda_k1, lambda_q2=lambda_q2, lambda_k2=lambda_k2,
             g_diff_subln=g_diff_subln, w_attn_out=w_attn_out, w_gla_in=w_gla_in, w_gla_gate=w_gla_gate,
             b_gla_gate=b_gla_gate, g_gla_out=g_gla_out, w_gla_out=w_gla_out, w_ffn_conv=w_ffn_conv,
             b_ffn_conv=b_ffn_conv)
    prep = {l: (_prep_attn(l // 2, W) if l % 2 == 0 else _prep_gla(l // 2, W)) for l in range(DEPTH)}
    prep['w_up'] = w_ffn_up.astype(BF16)
    prep['w_down'] = w_ffn_down.astype(BF16)

    bp, sp, d = x_prompt.shape
    bs, ts, _ = x_sample.shape
    n_pages = page_table.shape[1]
    past = n_pages * PAGE

    yp, p_lat, p_rope, p_k, p_v, p_gla, p_conv = _trunk(x_prompt, _rope_tables(jnp.arange(sp)), W, prep, None)

    la, n_pool = cache_mla_latent.shape[:2]
    sample = dict(
        lat=cache_mla_latent,
        ropeT=jnp.swapaxes(cache_mla_rope, 2, 3),
        dkT=jnp.transpose(cache_diff_k, (0, 1, 3, 4, 5, 2)),
        dv=cache_diff_v.reshape(la, n_pool, PAGE * DIFF_HEADS, DIFF_V),
        page_table=page_table, pps=next(p for p in (16, 8, 4, 2, 1) if n_pages % p == 0), nseq=bs,
        state_gla=state_gla, state_ffn_conv=state_ffn_conv)
    tab_s = jnp.tile(_rope_tables(past + jnp.arange(ts)), (bs, 1))
    ys, s_lat, s_rope, s_k, s_v, s_gla, s_conv = _trunk(x_sample.reshape(1, bs * ts, d), tab_s, W, prep, sample)

    def k6(kt, b, t):
        nb = kt.shape[1]
        kt = kt.reshape(kt.shape[0], nb, DIFF_HEADS, 2, DIFF_DH, b // nb, t)
        return jnp.transpose(kt, (0, 1, 5, 6, 2, 3, 4)).reshape(kt.shape[0], b, t, DIFF_HEADS, 2, DIFF_DH)

    def v5(a, b, t):
        return a.reshape(a.shape[0], b, t, DIFF_HEADS, DIFF_V)

    return (yp, ys.reshape(bs, ts, d),
            p_lat, p_rope, k6(p_k, bp, sp), v5(p_v, bp, sp), p_gla, p_conv,
            s_lat.reshape(-1, bs, ts, KV_LORA), s_rope.reshape(-1, bs, ts, MLA_ROPE), k6(s_k, bs, ts), v5(s_v, bs, ts),
            s_gla, s_conv)
```

```python
import functools
import math

import jax
import jax.numpy as jnp
from jax import lax
from jax.experimental import pallas as pl
from jax.experimental.pallas import tpu as pltpu

F32 = jnp.float32
BF16 = jnp.bfloat16

D_MODEL = 1024
DEPTH = 4
PAGE = 128
MLA_HEADS = 8
MLA_NOPE = 64
MLA_ROPE = 32
MLA_V = 64
Q_LORA = 256
KV_LORA = 128
DIFF_HEADS = 4
DIFF_DH = 64
DIFF_V = 128
GLA_HEADS = 4
GLA_DK = 128
GLA_DV = 256
GLA_RANK = 16
GLA_TAU = 16.0
D_FF = 2816
ROPE_THETA = 10000.0
EPS = 1e-6
NEG = -1e30
LOG2E = math.log2(math.e)
MLA_SCALE = (MLA_NOPE + MLA_ROPE) ** -0.5 * LOG2E
DIFF_SCALE = DIFF_DH ** -0.5 * LOG2E

LANES = 128
VMEM_LIMIT = 56 * 1024 * 1024

A_CQ, A_CKV, A_DQ, A_DK, A_DV, A_KR, A_W = 0, 256, 384, 896, 1408, 1920, 2048
G_Q, G_K, G_V, G_R, G_A, G_W = 0, 512, 1024, 2048, 3072, 3200

NT = (((1,), (1,)), ((), ()))
TN = (((0,), (0,)), ((), ()))


def _cp(*sem):
    return pltpu.CompilerParams(dimension_semantics=sem, vmem_limit_bytes=VMEM_LIMIT)


def _rms(x, g):
    return x * lax.rsqrt(jnp.mean(x * x, axis=-1, keepdims=True) + EPS) * g


def _dot(a, b):
    return jnp.dot(a, b, preferred_element_type=F32)


def _const_spec(shape):
    nd = len(shape)
    return pl.BlockSpec(shape, lambda *_: (0,) * nd)


def _rope_tables(pos):
    pos = pos.astype(F32)[:, None]
    lane = jnp.arange(LANES)

    def one(d):
        half = d // 2
        inv = ROPE_THETA ** (-jnp.arange(half, dtype=F32) * (2.0 / d))
        ang = pos * inv
        cos = jnp.tile(jnp.cos(ang), (1, LANES // half))
        sin = jnp.tile(jnp.sin(ang), (1, LANES // half))
        first = (lane % d) < half
        return [cos, jnp.where(first, -sin, 0.0), jnp.where(first, 0.0, sin)]

    return jnp.concatenate(one(DIFF_DH) + one(MLA_ROPE), axis=1)


def _rope(x, cos, sin_a, sin_b, half):
    return x * cos + pltpu.roll(x, LANES - half, 1) * sin_a + pltpu.roll(x, half, 1) * sin_b


def _attn_in_kernel(x_ref, g_ref, win_ref, gq_ref, wuq_ref, gkv_ref, wuk_ref, tab_ref, *rest):
    qm_ref, kcat_ref, dq_ref, dkb_ref, dvb_ref, ckv_ref, kr_ref, dkt_ref, dv4_ref = rest[-9:]
    tm = x_ref.shape[0]
    h = _rms(x_ref[...], g_ref[...]).astype(BF16)
    p = _dot(h, win_ref[...])
    tab = tab_ref[...]
    c64, a64, b64, c32, a32, b32 = [tab[:, i * LANES:(i + 1) * LANES] for i in range(6)]

    cqn = _rms(p[:, A_CQ:A_CKV], gq_ref[...]).astype(BF16)
    q = _dot(cqn, wuq_ref[...])
    nope_w = MLA_HEADS * MLA_NOPE
    qlat = _dot(q[:, :nope_w].astype(BF16), wuk_ref[...])
    for hh in range(MLA_HEADS):
        ql = qlat[:, hh * LANES:(hh + 1) * LANES]
        qr = _rope(q[:, nope_w + hh * LANES:nope_w + (hh + 1) * LANES], c32, a32, b32, MLA_ROPE // 2)
        qm_ref[hh] = (jnp.concatenate([ql, qr], axis=1) * MLA_SCALE).astype(qm_ref.dtype)

    ckv = _rms(p[:, A_CKV:A_DQ], gkv_ref[...])
    ckv_ref[...] = ckv
    krs = _rope(p[:, A_KR:A_W], c32, a32, b32, MLA_ROPE // 2)
    kr_ref[...] = krs[:, :MLA_ROPE]
    kcat_ref[...] = jnp.concatenate([ckv, krs], axis=1).astype(kcat_ref.dtype)
    for s in range(4):
        sl = slice(s * LANES, (s + 1) * LANES)
        dq = _rope(p[:, A_DQ + s * LANES:A_DQ + (s + 1) * LANES], c64, a64, b64, DIFF_DH // 2)
        dq_ref[:, sl] = (dq * DIFF_SCALE).astype(dq_ref.dtype)
        dk = _rope(p[:, A_DK + s * LANES:A_DK + (s + 1) * LANES], c64, a64, b64, DIFF_DH // 2)
        dkt_ref[sl, :] = dk.T
        dkb_ref[:, sl] = dk.astype(dkb_ref.dtype)
    dv = p[:, A_DV:A_KR]
    for hh in range(DIFF_HEADS):
        dv4_ref[pl.ds(hh, tm, stride=DIFF_HEADS), :] = dv[:, hh * LANES:(hh + 1) * LANES]
    dvb_ref[...] = dv.astype(dvb_ref.dtype)


def _attn_in(x, g, w, tab, act_dtype, tm, layer, bufs):
    nb, t, d = x.shape
    n_layers = bufs[0].shape[0]
    grid = (nb, t // tm)
    row = lambda c: pl.BlockSpec((None, tm, c), lambda b, i: (b, i, 0))
    lrow = lambda c: pl.BlockSpec((None, None, tm, c), lambda b, i: (layer, b, i, 0))
    out_shape = (
        jax.ShapeDtypeStruct((nb, MLA_HEADS, t, 2 * LANES), act_dtype),
        jax.ShapeDtypeStruct((nb, t, 2 * LANES), act_dtype),
        jax.ShapeDtypeStruct((nb, t, 512), act_dtype),
        jax.ShapeDtypeStruct((nb, t, 512), act_dtype),
        jax.ShapeDtypeStruct((nb, t, 512), act_dtype),
        jax.ShapeDtypeStruct((n_layers, nb, t, KV_LORA), F32),
        jax.ShapeDtypeStruct((n_layers, nb, t, MLA_ROPE), F32),
        jax.ShapeDtypeStruct((n_layers, nb, 512, t), F32),
        jax.ShapeDtypeStruct((n_layers, nb, t * DIFF_HEADS, DIFF_V), F32),
    )
    out_specs = (
        pl.BlockSpec((None, MLA_HEADS, tm, 2 * LANES), lambda b, i: (b, 0, i, 0)),
        row(2 * LANES), row(512), row(512), row(512), lrow(KV_LORA), lrow(MLA_ROPE),
        pl.BlockSpec((None, None, 512, tm), lambda b, i: (layer, b, 0, i)),
        pl.BlockSpec((None, None, tm * DIFF_HEADS, DIFF_V), lambda b, i: (layer, b, i, 0)),
    )
    in_specs = [
        row(d), _const_spec((1, d)), _const_spec(w['w_in'].shape), _const_spec((1, Q_LORA)),
        _const_spec(w['w_uq'].shape), _const_spec((1, KV_LORA)), _const_spec(w['w_ukbd'].shape),
        pl.BlockSpec((tm, 6 * LANES), lambda b, i: (i, 0)),
    ]
    args = [x, g, w['w_in'], w['g_q'], w['w_uq'], w['g_kv'], w['w_ukbd'], tab]
    aliases = {}
    for j, buf in enumerate(bufs):
        assert buf.shape == out_shape[5 + j].shape
        aliases[len(args)] = 5 + j
        in_specs.append(pl.BlockSpec(memory_space=pl.ANY))
        args.append(buf)
    return pl.pallas_call(
        _attn_in_kernel, grid=grid, in_specs=in_specs, out_specs=out_specs, out_shape=out_shape,
        input_output_aliases=aliases, compiler_params=_cp("parallel", "parallel"), name="attn_in",
    )(*args)


def _online_update(s, v, m_ref, l_ref, acc_ref):
    m_prev = m_ref[...]
    m_new = jnp.maximum(m_prev, jnp.max(s, axis=1, keepdims=True))
    alpha = jnp.exp2(m_prev - m_new)
    p = jnp.exp2(s - m_new)
    l_ref[...] = alpha * l_ref[...] + jnp.sum(p, axis=1, keepdims=True)
    acc_ref[...] = alpha * acc_ref[...] + _dot(p.astype(v.dtype), v)
    m_ref[...] = m_new


def _lambda(lam_ref, lam_init):
    lv = lam_ref[...]
    s1 = jnp.sum(lv[0:1] * lv[1:2], axis=1, keepdims=True)
    s2 = jnp.sum(lv[2:3] * lv[3:4], axis=1, keepdims=True)
    return jnp.exp(s1) - jnp.exp(s2) + lam_init


def _split_maps(dqh):
    lane = lax.broadcasted_iota(jnp.int32, dqh.shape, 1)
    zero = jnp.zeros_like(dqh)
    return jnp.concatenate([jnp.where(lane < DIFF_DH, dqh, zero), jnp.where(lane >= DIFF_DH, dqh, zero)], axis=0)


FLASH_TILES = MLA_HEADS + 2 * DIFF_HEADS
FLASH_KCHUNK = 512
FLASH_AHEAD = 2


def _flash_kernel(qm_ref, kcat_ref, dq_ref, dk_ref, dv_ref, lam_ref, olat_ref, odiff_ref,
                  dqs, m_ref, l_ref, acc_ref, *, tq, lam_init):
    qi = pl.program_id(1)
    m_ref[...] = jnp.full(m_ref.shape, NEG, F32)
    l_ref[...] = jnp.zeros(l_ref.shape, F32)
    acc_ref[...] = jnp.zeros(acc_ref.shape, F32)
    lane = lax.broadcasted_iota(jnp.int32, (tq, LANES), 1)
    for h in range(DIFF_HEADS):
        dqh = dq_ref[:, h * LANES:(h + 1) * LANES]
        zero = jnp.zeros_like(dqh)
        dqs[2 * h] = jnp.where(lane < DIFF_DH, dqh, zero)
        dqs[2 * h + 1] = jnp.where(lane >= DIFF_DH, dqh, zero)

    def chunk(k0, w, masked):
        kc = kcat_ref[pl.ds(k0, w), :]
        nl = w // LANES

        def scores(t):
            if t < MLA_HEADS:
                q, kk = qm_ref[t], kc
            else:
                h = (t - MLA_HEADS) // 2
                q = dqs[t - MLA_HEADS]
                kk = dk_ref[pl.ds(k0, w), h * LANES:(h + 1) * LANES]
            s = lax.dot_general(q, kk, NT, preferred_element_type=F32)
            if masked:
                row = lax.broadcasted_iota(jnp.int32, s.shape, 0)
                col = lax.broadcasted_iota(jnp.int32, s.shape, 1)
                s = jnp.where(col <= row, s, NEG)
            return s

        def update(t, s):
            if t < MLA_HEADS:
                v = kc[:, :KV_LORA]
            else:
                h = (t - MLA_HEADS) // 2
                v = dv_ref[pl.ds(k0, w), h * LANES:(h + 1) * LANES]
            mx = s[:, :LANES]
            for j in range(1, nl):
                mx = jnp.maximum(mx, s[:, j * LANES:(j + 1) * LANES])
            m_prev = m_ref[t]
            m_new = jnp.maximum(m_prev, jnp.max(mx, axis=1, keepdims=True))
            alpha = jnp.exp2(m_prev - m_new)
            ps = None
            pbs = []
            for j in range(nl):
                p = jnp.exp2(s[:, j * LANES:(j + 1) * LANES] - m_new)
                ps = p if ps is None else ps + p
                pbs.append(p.astype(BF16))
            l_ref[t] = alpha * l_ref[t] + ps
            acc_ref[t] = alpha * acc_ref[t] + _dot(jnp.concatenate(pbs, axis=1), v)
            m_ref[t] = m_new

        pend = [scores(t) for t in range(FLASH_AHEAD)]
        for t in range(FLASH_TILES):
            if t + FLASH_AHEAD < FLASH_TILES:
                pend.append(scores(t + FLASH_AHEAD))
            update(t, pend.pop(0))

    def body(j, carry):
        chunk(pl.multiple_of(j * FLASH_KCHUNK, FLASH_KCHUNK), FLASH_KCHUNK, False)
        return carry

    lax.fori_loop(0, (qi * tq) // FLASH_KCHUNK, body, 0)
    per = FLASH_KCHUNK // tq
    for r in range(1, per):
        @pl.when(qi % per >= r)
        def _():
            chunk(pl.multiple_of((qi - qi % per + (r - 1)) * tq, tq), tq, False)
    chunk(pl.multiple_of(qi * tq, tq), tq, True)

    lam = _lambda(lam_ref, lam_init)
    outs = [acc_ref[t] * (1.0 / jnp.sum(l_ref[t], axis=1, keepdims=True)) for t in range(FLASH_TILES)]
    for h in range(MLA_HEADS):
        olat_ref[:, h * LANES:(h + 1) * LANES] = outs[h].astype(olat_ref.dtype)
    for h in range(DIFF_HEADS):
        odiff_ref[:, h * LANES:(h + 1) * LANES] = outs[MLA_HEADS + 2 * h] - lam * outs[MLA_HEADS + 2 * h + 1]


def _flash(qm, kcat, dq, dk, dv, lam, lam_init, tq):
    nb, _, s, _ = qm.shape
    assert FLASH_KCHUNK % tq == 0 and s % tq == 0
    whole = lambda w: pl.BlockSpec((None, s, w), lambda b, qi: (b, 0, 0))
    in_specs = [
        pl.BlockSpec((None, MLA_HEADS, tq, 2 * LANES), lambda b, qi: (b, 0, qi, 0)),
        whole(2 * LANES),
        pl.BlockSpec((None, tq, 512), lambda b, qi: (b, qi, 0)),
        whole(512), whole(512), _const_spec((8, LANES)),
    ]
    out_specs = (pl.BlockSpec((None, tq, MLA_HEADS * LANES), lambda b, qi: (b, qi, 0)),
                 pl.BlockSpec((None, tq, 512), lambda b, qi: (b, qi, 0)))
    out_shape = (jax.ShapeDtypeStruct((nb, s, MLA_HEADS * LANES), BF16), jax.ShapeDtypeStruct((nb, s, 512), F32))
    scratch = [pltpu.VMEM((2 * DIFF_HEADS, tq, LANES), BF16), pltpu.VMEM((FLASH_TILES, tq, LANES), F32),
               pltpu.VMEM((FLASH_TILES, tq, LANES), F32), pltpu.VMEM((FLASH_TILES, tq, LANES), F32)]
    return pl.pallas_call(
        functools.partial(_flash_kernel, tq=tq, lam_init=lam_init), grid=(nb, s // tq), in_specs=in_specs,
        out_specs=out_specs, out_shape=out_shape, scratch_shapes=scratch,
        compiler_params=_cp("parallel", "arbitrary"), name="flash_prompt",
    )(qm, kcat, dq, dk, dv, lam)


DECODE_GROUPS = 2


def _decode_kernel(pt_ref, qm_ref, dq_ref, kcat_ref, dkn_ref, dvn_ref, lam_ref, *rest, pps, t_new, lam_init):
    lat_refs = rest[0 * pps:1 * pps]
    rope_refs = rest[1 * pps:2 * pps]
    dk_refs = rest[2 * pps:3 * pps]
    dv_refs = rest[3 * pps:4 * pps]
    olat_ref, odiff_ref, qs, qbd, m1, l1, acc1, m2, l2, acc2 = rest[4 * pps:]
    c = pl.program_id(1)
    nc = pl.num_programs(1)
    rows = MLA_HEADS * t_new
    drows = 2 * t_new

    @pl.when(c == 0)
    def _():
        qs[...] = qm_ref[...].reshape(rows, 2 * LANES)
        dq = dq_ref[...]
        zeros = jnp.zeros((drows, LANES), F32)
        for h in range(DIFF_HEADS):
            blk = _split_maps(dq[:, h * LANES:(h + 1) * LANES])
            qbd[h * drows:(h + 1) * drows, :] = jnp.concatenate(
                [blk if g == h else zeros for g in range(DIFF_HEADS)], axis=1)
        m1[...] = jnp.full(m1.shape, NEG, F32)
        l1[...] = jnp.zeros(l1.shape, F32)
        acc1[...] = jnp.zeros(acc1.shape, F32)
        m2[...] = jnp.full(m2.shape, NEG, F32)
        l2[...] = jnp.zeros(l2.shape, F32)
        acc2[...] = jnp.zeros(acc2.shape, F32)

    q = qs[...]
    ql = q[:, :KV_LORA].astype(BF16)
    qr = q[:, KV_LORA:KV_LORA + MLA_ROPE].astype(BF16)
    qd = qbd[...]
    qdb = qd.astype(BF16)

    def diag_blocks(wide):
        return jnp.concatenate([wide[h * drows:(h + 1) * drows, h * LANES:(h + 1) * LANES]
                                for h in range(DIFF_HEADS)], axis=0)

    def softmax_pv(s, pv_of, m_ref, l_ref, acc_ref):
        m_prev = m_ref[...]
        m_new = jnp.maximum(m_prev, jnp.max(s, axis=1, keepdims=True))
        alpha = jnp.exp2(m_prev - m_new)
        p = jnp.exp2(s - m_new)
        l_ref[...] = alpha * l_ref[...] + jnp.sum(p, axis=1, keepdims=True)
        acc_ref[...] = alpha * acc_ref[...] + pv_of(p)
        m_ref[...] = m_new

    def paged_pv(values, pages):
        def pv_of(p):
            pb = p.astype(BF16)
            pv = None
            for n, j in enumerate(pages):
                t = _dot(pb[:, n * PAGE:(n + 1) * PAGE], values(j))
                pv = t if pv is None else pv + t
            return pv
        return pv_of

    lat = [r[...].astype(BF16) for r in lat_refs]
    kt_rows = DIFF_HEADS * 2 * DIFF_DH

    def mla_scores(pages):
        return jnp.concatenate(
            [lax.dot_general(ql, lat[j], NT, preferred_element_type=F32) + _dot(qr, rope_refs[j][...].astype(BF16))
             for j in pages], axis=1)

    def diff_scores(pages):
        return jnp.concatenate(
            [_dot(qdb, dk_refs[j][...].reshape(kt_rows, PAGE).astype(BF16)) for j in pages], axis=1)

    def v_wide(j):
        return jnp.concatenate([dv_refs[j][pl.ds(h, PAGE, stride=DIFF_HEADS), :] for h in range(DIFF_HEADS)],
                               axis=1).astype(BF16)

    ngroups = DECODE_GROUPS if pps % DECODE_GROUPS == 0 else 1
    groups = [list(range(g * pps // ngroups, (g + 1) * pps // ngroups)) for g in range(ngroups)]
    pend = [(mla_scores(groups[0]), diff_scores(groups[0]))]
    for g, pages in enumerate(groups):
        if g + 1 < ngroups:
            pend.append((mla_scores(groups[g + 1]), diff_scores(groups[g + 1])))
        s_mla, s_diff = pend.pop(0)
        softmax_pv(s_mla, paged_pv(lambda j: lat[j], pages), m1, l1, acc1)
        softmax_pv(s_diff, lambda p: diag_blocks(paged_pv(v_wide, pages)(p)), m2, l2, acc2)

    @pl.when(c == nc - 1)
    def _():
        def causal(s):
            row = lax.broadcasted_iota(jnp.int32, s.shape, 0) & (t_new - 1)
            col = lax.broadcasted_iota(jnp.int32, s.shape, 1)
            return jnp.where(col <= row, s, NEG)

        kc = kcat_ref[...]
        s = causal(lax.dot_general(q, kc, NT, preferred_element_type=F32))
        softmax_pv(s, lambda p: _dot(p, kc[:, :KV_LORA]), m1, l1, acc1)
        o = acc1[...] / l1[...]
        for h in range(MLA_HEADS):
            olat_ref[:, h * LANES:(h + 1) * LANES] = o[h * t_new:(h + 1) * t_new].astype(olat_ref.dtype)
        s = causal(lax.dot_general(qd, dkn_ref[...], NT, preferred_element_type=F32))
        softmax_pv(s, lambda p: diag_blocks(_dot(p, dvn_ref[...])), m2, l2, acc2)
        o = acc2[...] / l2[...]
        lam = _lambda(lam_ref, lam_init)
        for h in range(DIFF_HEADS):
            oh = o[h * drows:(h + 1) * drows]
            odiff_ref[:, h * LANES:(h + 1) * LANES] = oh[:t_new] - lam * oh[t_new:]


def _decode(layer, qm, dq, kcat, dkn, dvn, lam, lam_init, lat_pool, ropeT_pool, dkT_pool, dv_pool, page_table, pps):
    _, _, tot, _ = qm.shape
    nbatch, n_pages = page_table.shape
    t_new = tot // nbatch
    nc = n_pages // pps
    rows = MLA_HEADS * t_new

    def page(j):
        return lambda b, c, pt: (layer, pt[b, c * pps + j]) + (0,) * 2

    def page4(j):
        return lambda b, c, pt: (layer, pt[b, c * pps + j], 0, 0, 0, 0)

    new = lambda w: pl.BlockSpec((None, t_new, w), lambda b, c, pt: (0, b, 0))
    in_specs = [
        pl.BlockSpec((None, MLA_HEADS, t_new, 2 * LANES), lambda b, c, pt: (0, 0, b, 0)),
        new(512), new(2 * LANES), new(512), new(512),
        pl.BlockSpec((8, LANES), lambda b, c, pt: (0, 0)),
    ]
    in_specs += [pl.BlockSpec((None, None, PAGE, KV_LORA), page(j)) for j in range(pps)]
    in_specs += [pl.BlockSpec((None, None, MLA_ROPE, PAGE), page(j)) for j in range(pps)]
    in_specs += [pl.BlockSpec((None, None, DIFF_HEADS, 2, DIFF_DH, PAGE), page4(j)) for j in range(pps)]
    in_specs += [pl.BlockSpec((None, None, PAGE * DIFF_HEADS, DIFF_V), page(j)) for j in range(pps)]
    out_specs = (new(MLA_HEADS * LANES), new(512))
    out_shape = (jax.ShapeDtypeStruct((1, tot, MLA_HEADS * LANES), F32),
                 jax.ShapeDtypeStruct((1, tot, 512), F32))
    drows = DIFF_HEADS * 2 * t_new
    scratch = [
        pltpu.VMEM((rows, 2 * LANES), F32), pltpu.VMEM((drows, DIFF_HEADS * LANES), F32),
        pltpu.VMEM((rows, 1), F32), pltpu.VMEM((rows, 1), F32), pltpu.VMEM((rows, KV_LORA), F32),
        pltpu.VMEM((drows, 1), F32), pltpu.VMEM((drows, 1), F32), pltpu.VMEM((drows, DIFF_V), F32),
    ]
    grid_spec = pltpu.PrefetchScalarGridSpec(
        num_scalar_prefetch=1, grid=(nbatch, nc), in_specs=in_specs, out_specs=out_specs, scratch_shapes=scratch)
    return pl.pallas_call(
        functools.partial(_decode_kernel, pps=pps, t_new=t_new, lam_init=lam_init),
        grid_spec=grid_spec, out_shape=out_shape,
        compiler_params=_cp("parallel", "arbitrary"), name="decode_attn",
    )(page_table, qm, dq, kcat, dkn, dvn, lam,
      *([lat_pool] * pps), *([ropeT_pool] * pps), *([dkT_pool] * pps), *([dv_pool] * pps))


def _attn_mix(x, olat_ref, odiff_ref, wuv_ref, gsub_ref, wo_ref, gpost_ref, scale):
    o_mla = _dot(olat_ref[...].astype(BF16), wuv_ref[...])
    od = odiff_ref[...]
    parts = [o_mla]
    for h in range(DIFF_HEADS):
        parts.append(_rms(od[:, h * LANES:(h + 1) * LANES], gsub_ref[...]) * scale)
    mix = _dot(jnp.concatenate(parts, axis=1).astype(BF16), wo_ref[...])
    return x + _rms(mix, gpost_ref[...])


def _gla_mix(x, o_ref, r_ref, gout_ref, wo_ref, gpost_ref):
    o = o_ref[...]
    parts = [_rms(o[:, h * GLA_DV:(h + 1) * GLA_DV], gout_ref[...]) for h in range(GLA_HEADS)]
    y = jnp.concatenate(parts, axis=1) * jax.nn.silu(r_ref[...])
    mix = _dot(y.astype(BF16), wo_ref[...])
    return x + _rms(mix, gpost_ref[...])


FFN_CHUNK = 256
FFN_AHEAD = 2
N_MIX_REFS = {'attn': 6, 'gla': 5}


def _ffn_kernel(x_ref, *refs, period, has_prev, mixer, scale):
    mix_refs, refs = refs[:N_MIX_REFS[mixer]], refs[N_MIX_REFS[mixer]:]
    if has_prev:
        gpre_ref, gpost_ref, wup_ref, wconv_ref, bconv_ref, wdown_ref, prev_ref, o_ref, st_ref, carry = refs
    else:
        gpre_ref, gpost_ref, wup_ref, wconv_ref, bconv_ref, wdown_ref, o_ref, st_ref, carry = refs
    tm = x_ref.shape[0]
    i = pl.program_id(1)
    if mixer == 'attn':
        x = _attn_mix(x_ref[...], *mix_refs, scale)
    else:
        x = _gla_mix(x_ref[...], *mix_refs)
    h = _rms(x, gpre_ref[...]).astype(BF16)
    row = lax.broadcasted_iota(jnp.int32, (tm, FFN_CHUNK), 0)
    t = row & (period - 1) if period < tm else row

    if not has_prev:
        @pl.when(i == 0)
        def _():
            carry[...] = jnp.zeros(carry.shape, F32)

    def up(c):
        return (_dot(h, wup_ref[:, c * FFN_CHUNK:(c + 1) * FFN_CHUNK]),
                _dot(h, wup_ref[:, D_FF + c * FFN_CHUNK:D_FF + (c + 1) * FFN_CHUNK]))

    nchunk = D_FF // FFN_CHUNK
    y = None
    pend = [up(c) for c in range(FFN_AHEAD)]
    for c in range(nchunk):
        sl = slice(c * FFN_CHUNK, (c + 1) * FFN_CHUNK)
        if c + FFN_AHEAD < nchunk:
            pend.append(up(c + FFN_AHEAD))
        a, v = pend.pop(0)
        if has_prev:
            nseq = tm // period
            pr = prev_ref[:, :, sl]
            p0 = jnp.broadcast_to(pr[:, 0:1, :], (nseq, period, FFN_CHUNK)).reshape(tm, FFN_CHUNK)
            p1 = jnp.broadcast_to(pr[:, 1:2, :], (nseq, period, FFN_CHUNK)).reshape(tm, FFN_CHUNK)
            st_ref[:, :, sl] = a.reshape(nseq, period, FFN_CHUNK)[:, period - 2:, :]
        else:
            cr = carry[:, sl]
            p0 = cr[6:7, :]
            p1 = cr[7:8, :]
            carry[:, sl] = a[tm - 8:, :]
            st_ref[:, sl] = a[tm - 2:, :]
        am1 = jnp.where(t == 0, p1, pltpu.roll(a, 1, 0))
        am2 = jnp.where(t == 0, p0, jnp.where(t == 1, p1, pltpu.roll(a, 2, 0)))
        wc = wconv_ref[:, sl]
        ac = bconv_ref[:, sl] + wc[0:1] * am2 + wc[1:2] * am1 + wc[2:3] * a
        gelu = 0.5 * ac * (1.0 + lax.erf(ac * math.sqrt(0.5)))
        z = (gelu * v).astype(BF16)
        yc = _dot(z, wdown_ref[sl, :])
        y = yc if y is None else y + yc
    o_ref[...] = x + _rms(y, gpost_ref[...])


def _ffn(x, mixer, mix_rows, mix_consts, scale, gpre, gpost, w_up, w_conv, b_conv, w_down, prev, tm):
    nb, t, d = x.shape
    has_prev = prev is not None
    row = pl.BlockSpec((None, tm, d), lambda b, i: (b, i, 0))
    rowc = lambda c: pl.BlockSpec((None, tm, c), lambda b, i: (b, i, 0))
    in_specs = [row] + [rowc(a.shape[-1]) for a in mix_rows] + [_const_spec(a.shape) for a in mix_consts]
    assert len(mix_rows) + len(mix_consts) == N_MIX_REFS[mixer]
    in_specs += [_const_spec((1, d)), _const_spec((1, d)), _const_spec(w_up.shape), _const_spec((3, D_FF)),
                 _const_spec((1, D_FF)), _const_spec(w_down.shape)]
    args = [x, *mix_rows, *mix_consts, gpre, gpost, w_up, w_conv, b_conv, w_down]
    if has_prev:
        nseq = prev.shape[0]
        period = t // nseq
        spt = tm // period
        in_specs.append(pl.BlockSpec((spt, 2, D_FF), lambda b, i: (i, 0, 0)))
        args.append(prev)
        st_spec = pl.BlockSpec((spt, 2, D_FF), lambda b, i: (i, 0, 0))
        st_shape = jax.ShapeDtypeStruct((nseq, 2, D_FF), F32)
    else:
        period = t
        st_spec = pl.BlockSpec((None, 2, D_FF), lambda b, i: (b, 0, 0))
        st_shape = jax.ShapeDtypeStruct((nb, 2, D_FF), F32)
    return pl.pallas_call(
        functools.partial(_ffn_kernel, period=period, has_prev=has_prev, mixer=mixer, scale=scale),
        grid=(nb, t // tm), in_specs=in_specs,
        out_specs=(row, st_spec), out_shape=(jax.ShapeDtypeStruct(x.shape, F32), st_shape),
        scratch_shapes=[pltpu.VMEM((8, D_FF), F32)],
        compiler_params=_cp("parallel", "arbitrary"), name="conv_ffn",
    )(*args)


def _gla_in_kernel(x_ref, g_ref, win_ref, wg_ref, bg_ref, q_ref, k_ref, v_ref, r_ref, ga_ref):
    h = _rms(x_ref[...], g_ref[...]).astype(BF16)
    a = _dot(h, win_ref[:, G_A:G_W])
    z = _dot(a.astype(BF16), wg_ref[...]) + bg_ref[...]
    q_ref[...] = _dot(h, win_ref[:, G_Q:G_K]) * (GLA_DK ** -0.5)
    ga_ref[...] = (jnp.minimum(z, 0.0) - jnp.log1p(jnp.exp(-jnp.abs(z)))) * (LOG2E / GLA_TAU)
    k_ref[...] = _dot(h, win_ref[:, G_K:G_V])
    v_ref[...] = _dot(h, win_ref[:, G_V:G_R]).astype(v_ref.dtype)
    r_ref[...] = _dot(h, win_ref[:, G_R:G_A])


def _gla_in(x, g, w, tm):
    nb, t, d = x.shape
    row = lambda c: pl.BlockSpec((None, tm, c), lambda b, i: (b, i, 0))
    sds = lambda c, dt: jax.ShapeDtypeStruct((nb, t, c), dt)
    return pl.pallas_call(
        _gla_in_kernel, grid=(nb, t // tm),
        in_specs=[row(d), _const_spec((1, d)), _const_spec(w['w_in'].shape), _const_spec(w['w_gate'].shape),
                  _const_spec((1, 512))],
        out_specs=(row(512), row(512), row(1024), row(1024), row(512)),
        out_shape=(sds(512, F32), sds(512, F32), sds(1024, BF16), sds(1024, F32), sds(512, F32)),
        compiler_params=_cp("parallel", "parallel"), name="gla_in",
    )(x, g, w['w_in'], w['w_gate'], w['b_gate'])


def _gla_chunk(q_ref, k_ref, v_ref, g_ref, b_ref, base, head, c, state, sub):
    mm = BF16 if c >= 16 else F32
    kcol = slice(head * GLA_DK, (head + 1) * GLA_DK)
    q = q_ref[base:base + c, kcol]
    k = k_ref[base:base + c, kcol]
    v = v_ref[base:base + c, head * GLA_DV:(head + 1) * GLA_DV].astype(mm)
    row = lax.broadcasted_iota(jnp.int32, (c, GLA_DK), 0)
    b = g_ref[base:base + c, kcol]
    step = 1
    while step < c:
        b = b + jnp.where(row >= step, pltpu.roll(b, step, 0), 0.0)
        step *= 2
    b_ref[...] = b
    b_end = b_ref[c - 1:c, :]
    lane_c = lax.broadcasted_iota(jnp.int32, (sub, c), 1)
    rsub = lax.broadcasted_iota(jnp.int32, (sub, c), 0)
    blocks = []
    for i in range(c // sub):
        r0 = i * sub
        qi = q[r0:r0 + sub]
        bi = b[r0:r0 + sub]
        d_i = jnp.zeros((sub, c), F32)
        for s in range(sub):
            r = r0 + s
            ks = k_ref[base + r:base + r + 1, kcol]
            bs = b_ref[r:r + 1, :]
            w = qi * ks * jnp.exp2(jnp.minimum(bi - bs, 0.0))
            col = jnp.sum(w, axis=1, keepdims=True)
            d_i = d_i + col * (lane_c[0:1] == r).astype(F32)
        a_i = jnp.where(lane_c - r0 <= rsub, d_i, 0.0)
        if i > 0:
            beta = b_ref[r0 - 1:r0, :]
            qt = (qi * jnp.exp2(bi - beta)).astype(mm)
            kt = (k * jnp.exp2(jnp.minimum(beta - b, 0.0))).astype(mm)
            below = lax.dot_general(qt, kt, NT, preferred_element_type=F32)
            a_i = jnp.where(lane_c < r0, below, a_i)
        blocks.append(a_i)
    a = blocks[0] if len(blocks) == 1 else jnp.concatenate(blocks, axis=0)
    qd = (q * jnp.exp2(b)).astype(mm)
    o = _dot(qd, state.astype(mm)) + _dot(a.astype(mm), v)
    kd = (k * jnp.exp2(b_end - b)).astype(mm)
    rr = lax.broadcasted_iota(jnp.int32, (GLA_DK, GLA_DK), 0)
    cc = lax.broadcasted_iota(jnp.int32, (GLA_DK, GLA_DK), 1)
    decay = jnp.sum(jnp.where(rr == cc, jnp.exp2(b_end), 0.0), axis=1, keepdims=True)
    new_state = decay * state + lax.dot_general(kd, v, TN, preferred_element_type=F32)
    return o, new_state


def _gla_kernel(*refs, chunk, sub, nseq, has_s0):
    if has_s0:
        q_ref, k_ref, v_ref, g_ref, s0_ref, _, o_ref, s_ref, st, b_scr = refs
    else:
        q_ref, k_ref, v_ref, g_ref, _, o_ref, s_ref, st, b_scr = refs

        @pl.when(pl.program_id(1) == 0)
        def _():
            st[...] = jnp.zeros(st.shape, F32)

    rows = q_ref.shape[0] // nseq
    for j in range(nseq):
        states = [s0_ref[j, h] if has_s0 else st[h] for h in range(GLA_HEADS)]
        for n in range(rows // chunk):
            base = j * rows + n * chunk
            for h in range(GLA_HEADS):
                o, states[h] = _gla_chunk(q_ref, k_ref, v_ref, g_ref, b_scr.at[h], base, h, chunk, states[h], sub)
                o_ref[base:base + chunk, h * GLA_DV:(h + 1) * GLA_DV] = o
        for h in range(GLA_HEADS):
            if not has_s0:
                st[h] = states[h]
            s_ref[j, h] = states[h]


def _gla(q, k, v, g, s0, s_buf, layer, chunk, sub, tt, nseq):
    nb, t, _ = q.shape
    has_s0 = s0 is not None
    assert (nb == 1) if has_s0 else (nseq == 1)
    blk = lambda w: pl.BlockSpec((None, tt, w), lambda b, i: (b, i, 0))
    st_blk = (None, nseq, GLA_HEADS, GLA_DK, GLA_DV)
    in_specs = [blk(GLA_HEADS * GLA_DK), blk(GLA_HEADS * GLA_DK), blk(GLA_HEADS * GLA_DV), blk(GLA_HEADS * GLA_DK)]
    args = [q, k, v, g]
    if has_s0:
        in_specs.append(pl.BlockSpec(st_blk, lambda b, i: (layer, i, 0, 0, 0)))
        args.append(s0)
        s_spec = pl.BlockSpec(st_blk, lambda b, i: (layer, i, 0, 0, 0))
        assert s_buf.shape[1] == (t // tt) * nseq
    else:
        s_spec = pl.BlockSpec(st_blk, lambda b, i: (layer, b, 0, 0, 0))
        assert s_buf.shape[1] == nb
    in_specs.append(pl.BlockSpec(memory_space=pl.ANY))
    args.append(s_buf)
    return pl.pallas_call(
        functools.partial(_gla_kernel, chunk=chunk, sub=sub, nseq=nseq, has_s0=has_s0),
        grid=(nb, t // tt), in_specs=in_specs, out_specs=(blk(GLA_HEADS * GLA_DV), s_spec),
        out_shape=(jax.ShapeDtypeStruct((nb, t, GLA_HEADS * GLA_DV), F32),
                   jax.ShapeDtypeStruct(s_buf.shape, F32)),
        input_output_aliases={len(args) - 1: 1},
        scratch_shapes=[pltpu.VMEM((GLA_HEADS, GLA_DK, GLA_DV), F32), pltpu.VMEM((GLA_HEADS, chunk, GLA_DK), F32)],
        compiler_params=_cp("parallel", "arbitrary"), name="gla_core",
    )(*args)


def _prep_attn(i, W):
    w_in = W['w_attn_in'][i]
    cq, ckv, kr, dq, dk, dv = jnp.split(w_in, [256, 384, 416, 928, 1440], axis=1)
    w_in2 = jnp.concatenate([cq, ckv, dq, dk, dv, kr, jnp.zeros((D_MODEL, A_W - A_KR - MLA_ROPE), F32)], axis=1)
    wq = W['w_uq'][i].reshape(Q_LORA, MLA_HEADS, MLA_NOPE + MLA_ROPE)
    w_nope = wq[:, :, :MLA_NOPE].reshape(Q_LORA, MLA_HEADS * MLA_NOPE)
    w_rope = jnp.pad(wq[:, :, MLA_NOPE:], ((0, 0), (0, 0), (0, LANES - MLA_ROPE))).reshape(Q_LORA, MLA_HEADS * LANES)
    eye = jnp.eye(MLA_HEADS, dtype=F32)
    w_ukbd = jnp.einsum('chn,hg->hngc', W['w_uk'][i], eye).reshape(MLA_HEADS * MLA_NOPE, MLA_HEADS * KV_LORA)
    w_uvbd = jnp.einsum('chv,hg->hcgv', W['w_uv'][i], eye).reshape(MLA_HEADS * KV_LORA, MLA_HEADS * MLA_V)
    lam = jnp.zeros((8, LANES), F32)
    for r, name in enumerate(('lambda_q1', 'lambda_k1', 'lambda_q2', 'lambda_k2')):
        lam = lam.at[r, :DIFF_DH].set(W[name][i])
    return dict(
        w_in=w_in2.astype(BF16), g_q=W['g_q_norm'][i][None], w_uq=jnp.concatenate([w_nope, w_rope], 1).astype(BF16),
        g_kv=W['g_kv_norm'][i][None], w_ukbd=w_ukbd.astype(BF16), w_uvbd=w_uvbd.astype(BF16),
        g_sub=W['g_diff_subln'][i][None], w_out=W['w_attn_out'][i].astype(BF16), lam=lam)


def _prep_gla(i, W):
    w_in = W['w_gla_in'][i]
    q, k, v, a, r = jnp.split(w_in, [512, 1024, 2048, 2064], axis=1)
    w_in2 = jnp.concatenate([q, k, v, r, a, jnp.zeros((D_MODEL, G_W - G_A - GLA_RANK), F32)], axis=1)
    w_gate = jnp.pad(W['w_gla_gate'][i], ((0, LANES - GLA_RANK), (0, 0)))
    return dict(w_in=w_in2.astype(BF16), w_gate=w_gate.astype(BF16), b_gate=W['b_gla_gate'][i][None],
                g_out=W['g_gla_out'][i][None], w_out=W['w_gla_out'][i].astype(BF16))


def _tile(t, pref):
    return pref if t % pref == 0 else t


def _trunk(x, tab, W, prep, sample):
    nb, t, d = x.shape
    tm = _tile(t, 512)
    act = F32 if sample is not None else BF16
    n_attn, n_gla = (DEPTH + 1) // 2, DEPTH // 2
    caches = (jnp.zeros((n_attn, nb, t, KV_LORA), F32), jnp.zeros((n_attn, nb, t, MLA_ROPE), F32),
              jnp.zeros((n_attn, nb, 512, t), F32), jnp.zeros((n_attn, nb, t * DIFF_HEADS, DIFF_V), F32))
    gla = jnp.zeros((n_gla, nb if sample is None else sample['nseq'], GLA_HEADS, GLA_DK, GLA_DV), F32)
    conv = []
    for l in range(DEPTH):
        i = l // 2
        g_pre = W['g_mix_pre'][l][None]
        g_post = W['g_mix_post'][l][None]
        w = prep[l]
        if l % 2 == 0:
            lam_init = 0.8 - 0.6 * math.exp(-0.3 * l)
            qm, kcat, dq, dkb, dvb, *caches = _attn_in(x, g_pre, w, tab, act, tm, i, caches)
            if sample is None:
                olat, odiff = _flash(qm, kcat, dq, dkb, dvb, w['lam'], lam_init, _tile(t, 256))
            else:
                olat, odiff = _decode(i, qm, dq, kcat, dkb, dvb, w['lam'], lam_init, sample['lat'], sample['ropeT'],
                                      sample['dkT'], sample['dv'], sample['page_table'], sample['pps'])
            mix = ('attn', (olat, odiff), (w['w_uvbd'], w['g_sub'], w['w_out'], g_post), 1.0 - lam_init)
        else:
            q, k, v, r, ga = _gla_in(x, g_pre, w, tm)
            if sample is None:
                o, gla = _gla(q, k, v, ga, None, gla, i, 64, 16, _tile(t, 256), 1)
            else:
                t_new = t // sample['nseq']
                per_step = 8 if sample['nseq'] % 8 == 0 else 1
                o, gla = _gla(q, k, v, ga, sample['state_gla'], gla, i, t_new, t_new, per_step * t_new, per_step)
            mix = ('gla', (o, r), (w['g_out'], w['w_out'], g_post), None)
        prev = None if sample is None else sample['state_ffn_conv'][l]
        x, c = _ffn(x, *mix, W['g_ffn_pre'][l][None], W['g_ffn_post'][l][None], prep['w_up'][l], W['w_ffn_conv'][l],
                    W['b_ffn_conv'][l][None], prep['w_down'][l], prev, _tile(t, 256))
        conv.append(c)
    return (x,) + tuple(caches) + (gla, jnp.stack(conv))


def kernel(x_prompt, x_sample, cache_mla_latent, cache_mla_rope, cache_diff_k, cache_diff_v, state_gla, state_ffn_conv, page_table, g_mix_pre, g_mix_post, g_ffn_pre, g_ffn_post, w_attn_in, g_q_norm, w_uq, g_kv_norm, w_uk, w_uv, lambda_q1, lambda_k1, lambda_q2, lambda_k2, g_diff_subln, w_attn_out, w_gla_in, w_gla_gate, b_gla_gate, g_gla_out, w_gla_out, w_ffn_up, w_ffn_conv, b_ffn_conv, w_ffn_down):
    W = dict(g_mix_pre=g_mix_pre, g_mix_post=g_mix_post, g_ffn_pre=g_ffn_pre, g_ffn_post=g_ffn_post,
             w_attn_in=w_attn_in, g_q_norm=g_q_norm, w_uq=w_uq, g_kv_norm=g_kv_norm, w_uk=w_uk, w_uv=w_uv,
             lambda_q1=lambda_q1, lambda_k1=lambda_k1, lambda_q2=lambda_q2, lambda_k2=lambda_k2,
             g_diff_subln=g_diff_subln, w_attn_out=w_attn_out, w_gla_in=w_gla_in, w_gla_gate=w_gla_gate,
             b_gla_gate=b_gla_gate, g_gla_out=g_gla_out, w_gla_out=w_gla_out, w_ffn_conv=w_ffn_conv,
             b_ffn_conv=b_ffn_conv)
    prep = {l: (_prep_attn(l // 2, W) if l % 2 == 0 else _prep_gla(l // 2, W)) for l in range(DEPTH)}
    prep['w_up'] = w_ffn_up.astype(BF16)
    prep['w_down'] = w_ffn_down.astype(BF16)

    bp, sp, d = x_prompt.shape
    bs, ts, _ = x_sample.shape
    n_pages = page_table.shape[1]
    past = n_pages * PAGE

    yp, p_lat, p_rope, p_k, p_v, p_gla, p_conv = _trunk(x_prompt, _rope_tables(jnp.arange(sp)), W, prep, None)

    la, n_pool = cache_mla_latent.shape[:2]
    sample = dict(
        lat=cache_mla_latent,
        ropeT=jnp.swapaxes(cache_mla_rope, 2, 3),
        dkT=jnp.transpose(cache_diff_k, (0, 1, 3, 4, 5, 2)),
        dv=cache_diff_v.reshape(la, n_pool, PAGE * DIFF_HEADS, DIFF_V),
        page_table=page_table, pps=next(p for p in (32, 16, 8, 4, 2, 1) if n_pages % p == 0), nseq=bs,
        state_gla=state_gla, state_ffn_conv=state_ffn_conv)
    tab_s = jnp.tile(_rope_tables(past + jnp.arange(ts)), (bs, 1))
    ys, s_lat, s_rope, s_k, s_v, s_gla, s_conv = _trunk(x_sample.reshape(1, bs * ts, d), tab_s, W, prep, sample)

    def k6(kt, b, t):
        nb = kt.shape[1]
        kt = kt.reshape(kt.shape[0], nb, DIFF_HEADS, 2, DIFF_DH, b // nb, t)
        return jnp.transpose(kt, (0, 1, 5, 6, 2, 3, 4)).reshape(kt.shape[0], b, t, DIFF_HEADS, 2, DIFF_DH)

    def v5(a, b, t):
        return a.reshape(a.shape[0], b, t, DIFF_HEADS, DIFF_V)

    return (yp, ys.reshape(bs, ts, d),
            p_lat, p_rope, k6(p_k, bp, sp), v5(p_v, bp, sp), p_gla, p_conv,
            s_lat.reshape(-1, bs, ts, KV_LORA), s_rope.reshape(-1, bs, ts, MLA_ROPE), k6(s_k, bs, ts), v5(s_v, bs, ts),
            s_gla, s_conv)
```

```python
import functools
import math

import jax
import jax.numpy as jnp
from jax import lax
from jax.experimental import pallas as pl
from jax.experimental.pallas import tpu as pltpu

F32 = jnp.float32
BF16 = jnp.bfloat16

D_MODEL = 1024
DEPTH = 4
PAGE = 128
MLA_HEADS = 8
MLA_NOPE = 64
MLA_ROPE = 32
MLA_V = 64
Q_LORA = 256
KV_LORA = 128
DIFF_HEADS = 4
DIFF_DH = 64
DIFF_V = 128
GLA_HEADS = 4
GLA_DK = 128
GLA_DV = 256
GLA_RANK = 16
GLA_TAU = 16.0
D_FF = 2816
ROPE_THETA = 10000.0
EPS = 1e-6
NEG = -1e30
LOG2E = math.log2(math.e)
MLA_SCALE = (MLA_NOPE + MLA_ROPE) ** -0.5 * LOG2E
DIFF_SCALE = DIFF_DH ** -0.5 * LOG2E

LANES = 128
VMEM_LIMIT = 56 * 1024 * 1024

A_CQ, A_CKV, A_DQ, A_DK, A_DV, A_KR, A_W = 0, 256, 384, 896, 1408, 1920, 2048
G_Q, G_K, G_V, G_R, G_A, G_W = 0, 512, 1024, 2048, 3072, 3200

NT = (((1,), (1,)), ((), ()))
TN = (((0,), (0,)), ((), ()))


def _cp(*sem):
    return pltpu.CompilerParams(dimension_semantics=sem, vmem_limit_bytes=VMEM_LIMIT)


def _rms(x, g):
    return x * lax.rsqrt(jnp.mean(x * x, axis=-1, keepdims=True) + EPS) * g


def _dot(a, b):
    return jnp.dot(a, b, preferred_element_type=F32)


def _const_spec(shape):
    nd = len(shape)
    return pl.BlockSpec(shape, lambda *_: (0,) * nd)


def _rope_tables(pos):
    pos = pos.astype(F32)[:, None]
    lane = jnp.arange(LANES)

    def one(d):
        half = d // 2
        inv = ROPE_THETA ** (-jnp.arange(half, dtype=F32) * (2.0 / d))
        ang = pos * inv
        cos = jnp.tile(jnp.cos(ang), (1, LANES // half))
        sin = jnp.tile(jnp.sin(ang), (1, LANES // half))
        first = (lane % d) < half
        return [cos, jnp.where(first, -sin, 0.0), jnp.where(first, 0.0, sin)]

    return jnp.concatenate(one(DIFF_DH) + one(MLA_ROPE), axis=1)


def _rope(x, cos, sin_a, sin_b, half):
    return x * cos + pltpu.roll(x, LANES - half, 1) * sin_a + pltpu.roll(x, half, 1) * sin_b


def _attn_in_kernel(x_ref, g_ref, win_ref, gq_ref, wuq_ref, gkv_ref, wuk_ref, tab_ref, *rest):
    qm_ref, kcat_ref, dq_ref, dkb_ref, dvb_ref, ckv_ref, kr_ref, dkt_ref, dv4_ref = rest[-9:]
    tm = x_ref.shape[0]
    h = _rms(x_ref[...], g_ref[...]).astype(BF16)
    p = _dot(h, win_ref[...])
    tab = tab_ref[...]
    c64, a64, b64, c32, a32, b32 = [tab[:, i * LANES:(i + 1) * LANES] for i in range(6)]

    cqn = _rms(p[:, A_CQ:A_CKV], gq_ref[...]).astype(BF16)
    q = _dot(cqn, wuq_ref[...])
    nope_w = MLA_HEADS * MLA_NOPE
    qlat = _dot(q[:, :nope_w].astype(BF16), wuk_ref[...])
    for hh in range(MLA_HEADS):
        ql = qlat[:, hh * LANES:(hh + 1) * LANES]
        qr = _rope(q[:, nope_w + hh * LANES:nope_w + (hh + 1) * LANES], c32, a32, b32, MLA_ROPE // 2)
        qm_ref[hh] = (jnp.concatenate([ql, qr], axis=1) * MLA_SCALE).astype(qm_ref.dtype)

    ckv = _rms(p[:, A_CKV:A_DQ], gkv_ref[...])
    ckv_ref[...] = ckv
    krs = _rope(p[:, A_KR:A_W], c32, a32, b32, MLA_ROPE // 2)
    kr_ref[...] = krs[:, :MLA_ROPE]
    kcat_ref[...] = jnp.concatenate([ckv, krs], axis=1).astype(kcat_ref.dtype)
    for s in range(4):
        sl = slice(s * LANES, (s + 1) * LANES)
        dq = _rope(p[:, A_DQ + s * LANES:A_DQ + (s + 1) * LANES], c64, a64, b64, DIFF_DH // 2)
        dq_ref[:, sl] = (dq * DIFF_SCALE).astype(dq_ref.dtype)
        dk = _rope(p[:, A_DK + s * LANES:A_DK + (s + 1) * LANES], c64, a64, b64, DIFF_DH // 2)
        dkt_ref[sl, :] = dk.T
        dkb_ref[:, sl] = dk.astype(dkb_ref.dtype)
    dv = p[:, A_DV:A_KR]
    for hh in range(DIFF_HEADS):
        dv4_ref[pl.ds(hh, tm, stride=DIFF_HEADS), :] = dv[:, hh * LANES:(hh + 1) * LANES]
    dvb_ref[...] = dv.astype(dvb_ref.dtype)


def _attn_in(x, g, w, tab, act_dtype, tm, layer, bufs):
    nb, t, d = x.shape
    n_layers = bufs[0].shape[0]
    grid = (nb, t // tm)
    row = lambda c: pl.BlockSpec((None, tm, c), lambda b, i: (b, i, 0))
    lrow = lambda c: pl.BlockSpec((None, None, tm, c), lambda b, i: (layer, b, i, 0))
    out_shape = (
        jax.ShapeDtypeStruct((nb, MLA_HEADS, t, 2 * LANES), act_dtype),
        jax.ShapeDtypeStruct((nb, t, 2 * LANES), act_dtype),
        jax.ShapeDtypeStruct((nb, t, 512), act_dtype),
        jax.ShapeDtypeStruct((nb, t, 512), act_dtype),
        jax.ShapeDtypeStruct((nb, t, 512), act_dtype),
        jax.ShapeDtypeStruct((n_layers, nb, t, KV_LORA), F32),
        jax.ShapeDtypeStruct((n_layers, nb, t, MLA_ROPE), F32),
        jax.ShapeDtypeStruct((n_layers, nb, 512, t), F32),
        jax.ShapeDtypeStruct((n_layers, nb, t * DIFF_HEADS, DIFF_V), F32),
    )
    out_specs = (
        pl.BlockSpec((None, MLA_HEADS, tm, 2 * LANES), lambda b, i: (b, 0, i, 0)),
        row(2 * LANES), row(512), row(512), row(512), lrow(KV_LORA), lrow(MLA_ROPE),
        pl.BlockSpec((None, None, 512, tm), lambda b, i: (layer, b, 0, i)),
        pl.BlockSpec((None, None, tm * DIFF_HEADS, DIFF_V), lambda b, i: (layer, b, i, 0)),
    )
    in_specs = [
        row(d), _const_spec((1, d)), _const_spec(w['w_in'].shape), _const_spec((1, Q_LORA)),
        _const_spec(w['w_uq'].shape), _const_spec((1, KV_LORA)), _const_spec(w['w_ukbd'].shape),
        pl.BlockSpec((tm, 6 * LANES), lambda b, i: (i, 0)),
    ]
    args = [x, g, w['w_in'], w['g_q'], w['w_uq'], w['g_kv'], w['w_ukbd'], tab]
    aliases = {}
    for j, buf in enumerate(bufs):
        assert buf.shape == out_shape[5 + j].shape
        aliases[len(args)] = 5 + j
        in_specs.append(pl.BlockSpec(memory_space=pl.ANY))
        args.append(buf)
    return pl.pallas_call(
        _attn_in_kernel, grid=grid, in_specs=in_specs, out_specs=out_specs, out_shape=out_shape,
        input_output_aliases=aliases, compiler_params=_cp("parallel", "parallel"), name="attn_in",
    )(*args)


def _online_update(s, v, m_ref, l_ref, acc_ref):
    m_prev = m_ref[...]
    m_new = jnp.maximum(m_prev, jnp.max(s, axis=1, keepdims=True))
    alpha = jnp.exp2(m_prev - m_new)
    p = jnp.exp2(s - m_new)
    l_ref[...] = alpha * l_ref[...] + jnp.sum(p, axis=1, keepdims=True)
    acc_ref[...] = alpha * acc_ref[...] + _dot(p.astype(v.dtype), v)
    m_ref[...] = m_new


def _lambda(lam_ref, lam_init):
    lv = lam_ref[...]
    s1 = jnp.sum(lv[0:1] * lv[1:2], axis=1, keepdims=True)
    s2 = jnp.sum(lv[2:3] * lv[3:4], axis=1, keepdims=True)
    return jnp.exp(s1) - jnp.exp(s2) + lam_init


def _split_maps(dqh):
    lane = lax.broadcasted_iota(jnp.int32, dqh.shape, 1)
    zero = jnp.zeros_like(dqh)
    return jnp.concatenate([jnp.where(lane < DIFF_DH, dqh, zero), jnp.where(lane >= DIFF_DH, dqh, zero)], axis=0)


FLASH_TILES = MLA_HEADS + 2 * DIFF_HEADS
FLASH_KCHUNK = 512
FLASH_AHEAD = 2


def _flash_kernel(qm_ref, kcat_ref, dq_ref, dk_ref, dv_ref, lam_ref, olat_ref, odiff_ref,
                  dqs, m_ref, l_ref, acc_ref, *, tq, lam_init):
    qi = pl.program_id(1)
    m_ref[...] = jnp.full(m_ref.shape, NEG, F32)
    l_ref[...] = jnp.zeros(l_ref.shape, F32)
    acc_ref[...] = jnp.zeros(acc_ref.shape, F32)
    lane = lax.broadcasted_iota(jnp.int32, (tq, LANES), 1)
    for h in range(DIFF_HEADS):
        dqh = dq_ref[:, h * LANES:(h + 1) * LANES]
        zero = jnp.zeros_like(dqh)
        dqs[2 * h] = jnp.where(lane < DIFF_DH, dqh, zero)
        dqs[2 * h + 1] = jnp.where(lane >= DIFF_DH, dqh, zero)

    def chunk(k0, w, masked):
        kc = kcat_ref[pl.ds(k0, w), :]
        nl = w // LANES

        def scores(t):
            if t < MLA_HEADS:
                q, kk = qm_ref[t], kc
            else:
                h = (t - MLA_HEADS) // 2
                q = dqs[t - MLA_HEADS]
                kk = dk_ref[pl.ds(k0, w), h * LANES:(h + 1) * LANES]
            s = lax.dot_general(q, kk, NT, preferred_element_type=F32)
            if masked:
                row = lax.broadcasted_iota(jnp.int32, s.shape, 0)
                col = lax.broadcasted_iota(jnp.int32, s.shape, 1)
                s = jnp.where(col <= row, s, NEG)
            return s

        def update(t, s):
            if t < MLA_HEADS:
                v = kc[:, :KV_LORA]
            else:
                h = (t - MLA_HEADS) // 2
                v = dv_ref[pl.ds(k0, w), h * LANES:(h + 1) * LANES]
            mx = s[:, :LANES]
            for j in range(1, nl):
                mx = jnp.maximum(mx, s[:, j * LANES:(j + 1) * LANES])
            m_prev = m_ref[t]
            m_new = jnp.maximum(m_prev, jnp.max(mx, axis=1, keepdims=True))
            alpha = jnp.exp2(m_prev - m_new)
            ps = None
            pbs = []
            for j in range(nl):
                p = jnp.exp2(s[:, j * LANES:(j + 1) * LANES] - m_new)
                ps = p if ps is None else ps + p
                pbs.append(p.astype(BF16))
            l_ref[t] = alpha * l_ref[t] + ps
            acc_ref[t] = alpha * acc_ref[t] + _dot(jnp.concatenate(pbs, axis=1), v)
            m_ref[t] = m_new

        pend = [scores(t) for t in range(FLASH_AHEAD)]
        for t in range(FLASH_TILES):
            if t + FLASH_AHEAD < FLASH_TILES:
                pend.append(scores(t + FLASH_AHEAD))
            update(t, pend.pop(0))

    def body(j, carry):
        chunk(pl.multiple_of(j * FLASH_KCHUNK, FLASH_KCHUNK), FLASH_KCHUNK, False)
        return carry

    lax.fori_loop(0, (qi * tq) // FLASH_KCHUNK, body, 0)
    per = FLASH_KCHUNK // tq
    for r in range(1, per):
        @pl.when(qi % per >= r)
        def _():
            chunk(pl.multiple_of((qi - qi % per + (r - 1)) * tq, tq), tq, False)
    chunk(pl.multiple_of(qi * tq, tq), tq, True)

    lam = _lambda(lam_ref, lam_init)
    outs = [acc_ref[t] * (1.0 / jnp.sum(l_ref[t], axis=1, keepdims=True)) for t in range(FLASH_TILES)]
    for h in range(MLA_HEADS):
        olat_ref[:, h * LANES:(h + 1) * LANES] = outs[h].astype(olat_ref.dtype)
    for h in range(DIFF_HEADS):
        odiff_ref[:, h * LANES:(h + 1) * LANES] = outs[MLA_HEADS + 2 * h] - lam * outs[MLA_HEADS + 2 * h + 1]


def _flash(qm, kcat, dq, dk, dv, lam, lam_init, tq):
    nb, _, s, _ = qm.shape
    assert FLASH_KCHUNK % tq == 0 and s % tq == 0
    whole = lambda w: pl.BlockSpec((None, s, w), lambda b, qi: (b, 0, 0))
    in_specs = [
        pl.BlockSpec((None, MLA_HEADS, tq, 2 * LANES), lambda b, qi: (b, 0, qi, 0)),
        whole(2 * LANES),
        pl.BlockSpec((None, tq, 512), lambda b, qi: (b, qi, 0)),
        whole(512), whole(512), _const_spec((8, LANES)),
    ]
    out_specs = (pl.BlockSpec((None, tq, MLA_HEADS * LANES), lambda b, qi: (b, qi, 0)),
                 pl.BlockSpec((None, tq, 512), lambda b, qi: (b, qi, 0)))
    out_shape = (jax.ShapeDtypeStruct((nb, s, MLA_HEADS * LANES), BF16), jax.ShapeDtypeStruct((nb, s, 512), F32))
    scratch = [pltpu.VMEM((2 * DIFF_HEADS, tq, LANES), BF16), pltpu.VMEM((FLASH_TILES, tq, LANES), F32),
               pltpu.VMEM((FLASH_TILES, tq, LANES), F32), pltpu.VMEM((FLASH_TILES, tq, LANES), F32)]
    return pl.pallas_call(
        functools.partial(_flash_kernel, tq=tq, lam_init=lam_init), grid=(nb, s // tq), in_specs=in_specs,
        out_specs=out_specs, out_shape=out_shape, scratch_shapes=scratch,
        compiler_params=_cp("parallel", "arbitrary"), name="flash_prompt",
    )(qm, kcat, dq, dk, dv, lam)


DECODE_GROUPS = 2


def _decode_kernel(pt_ref, qm_ref, dq_ref, kcat_ref, dkn_ref, dvn_ref, lam_ref, *rest, pps, t_new, lam_init):
    lat_refs = rest[0 * pps:1 * pps]
    rope_refs = rest[1 * pps:2 * pps]
    dk_refs = rest[2 * pps:3 * pps]
    dv_refs = rest[3 * pps:4 * pps]
    olat_ref, odiff_ref, qs, qbd, m1, l1, acc1, m2, l2, acc2 = rest[4 * pps:]
    c = pl.program_id(1)
    nc = pl.num_programs(1)
    rows = MLA_HEADS * t_new
    drows = 2 * t_new

    @pl.when(c == 0)
    def _():
        qs[...] = qm_ref[...].reshape(rows, 2 * LANES)
        dq = dq_ref[...]
        zeros = jnp.zeros((drows, LANES), F32)
        for h in range(DIFF_HEADS):
            blk = _split_maps(dq[:, h * LANES:(h + 1) * LANES])
            qbd[h * drows:(h + 1) * drows, :] = jnp.concatenate(
                [blk if g == h else zeros for g in range(DIFF_HEADS)], axis=1)
        m1[...] = jnp.full(m1.shape, NEG, F32)
        l1[...] = jnp.zeros(l1.shape, F32)
        acc1[...] = jnp.zeros(acc1.shape, F32)
        m2[...] = jnp.full(m2.shape, NEG, F32)
        l2[...] = jnp.zeros(l2.shape, F32)
        acc2[...] = jnp.zeros(acc2.shape, F32)

    q = qs[...]
    ql = q[:, :KV_LORA].astype(BF16)
    qr = q[:, KV_LORA:KV_LORA + MLA_ROPE].astype(BF16)
    qd = qbd[...]
    qdb = qd.astype(BF16)

    def diag_blocks(wide):
        return jnp.concatenate([wide[h * drows:(h + 1) * drows, h * LANES:(h + 1) * LANES]
                                for h in range(DIFF_HEADS)], axis=0)

    def softmax_pv(s, pv_of, m_ref, l_ref, acc_ref):
        m_prev = m_ref[...]
        m_new = jnp.maximum(m_prev, jnp.max(s, axis=1, keepdims=True))
        alpha = jnp.exp2(m_prev - m_new)
        p = jnp.exp2(s - m_new)
        l_ref[...] = alpha * l_ref[...] + jnp.sum(p, axis=1, keepdims=True)
        acc_ref[...] = alpha * acc_ref[...] + pv_of(p)
        m_ref[...] = m_new

    def paged_pv(values, pages):
        def pv_of(p):
            pb = p.astype(BF16)
            pv = None
            for n, j in enumerate(pages):
                t = _dot(pb[:, n * PAGE:(n + 1) * PAGE], values(j))
                pv = t if pv is None else pv + t
            return pv
        return pv_of

    lat = [r[...].astype(BF16) for r in lat_refs]
    kt_rows = DIFF_HEADS * 2 * DIFF_DH

    def mla_scores(pages):
        return jnp.concatenate(
            [lax.dot_general(ql, lat[j], NT, preferred_element_type=F32) + _dot(qr, rope_refs[j][...].astype(BF16))
             for j in pages], axis=1)

    def diff_scores(pages):
        return jnp.concatenate(
            [_dot(qdb, dk_refs[j][...].reshape(kt_rows, PAGE).astype(BF16)) for j in pages], axis=1)

    def v_wide(j):
        return jnp.concatenate([dv_refs[j][pl.ds(h, PAGE, stride=DIFF_HEADS), :] for h in range(DIFF_HEADS)],
                               axis=1).astype(BF16)

    ngroups = DECODE_GROUPS if pps % DECODE_GROUPS == 0 else 1
    groups = [list(range(g * pps // ngroups, (g + 1) * pps // ngroups)) for g in range(ngroups)]
    pend = [(mla_scores(groups[0]), diff_scores(groups[0]))]
    for g, pages in enumerate(groups):
        if g + 1 < ngroups:
            pend.append((mla_scores(groups[g + 1]), diff_scores(groups[g + 1])))
        s_mla, s_diff = pend.pop(0)
        softmax_pv(s_mla, paged_pv(lambda j: lat[j], pages), m1, l1, acc1)
        softmax_pv(s_diff, lambda p: diag_blocks(paged_pv(v_wide, pages)(p)), m2, l2, acc2)

    @pl.when(c == nc - 1)
    def _():
        def causal(s):
            row = lax.broadcasted_iota(jnp.int32, s.shape, 0) & (t_new - 1)
            col = lax.broadcasted_iota(jnp.int32, s.shape, 1)
            return jnp.where(col <= row, s, NEG)

        kc = kcat_ref[...]
        s = causal(lax.dot_general(q, kc, NT, preferred_element_type=F32))
        softmax_pv(s, lambda p: _dot(p, kc[:, :KV_LORA]), m1, l1, acc1)
        o = acc1[...] / l1[...]
        for h in range(MLA_HEADS):
            olat_ref[:, h * LANES:(h + 1) * LANES] = o[h * t_new:(h + 1) * t_new].astype(olat_ref.dtype)
        s = causal(lax.dot_general(qd, dkn_ref[...], NT, preferred_element_type=F32))
        softmax_pv(s, lambda p: diag_blocks(_dot(p, dvn_ref[...])), m2, l2, acc2)
        o = acc2[...] / l2[...]
        lam = _lambda(lam_ref, lam_init)
        for h in range(DIFF_HEADS):
            oh = o[h * drows:(h + 1) * drows]
            odiff_ref[:, h * LANES:(h + 1) * LANES] = oh[:t_new] - lam * oh[t_new:]


def _decode(layer, qm, dq, kcat, dkn, dvn, lam, lam_init, lat_pool, ropeT_pool, dkT_pool, dv_pool, page_table, pps):
    _, _, tot, _ = qm.shape
    nbatch, n_pages = page_table.shape
    t_new = tot // nbatch
    nc = n_pages // pps
    rows = MLA_HEADS * t_new

    def page(j):
        return lambda b, c, pt: (layer, pt[b, c * pps + j]) + (0,) * 2

    def page4(j):
        return lambda b, c, pt: (layer, pt[b, c * pps + j], 0, 0, 0, 0)

    new = lambda w: pl.BlockSpec((None, t_new, w), lambda b, c, pt: (0, b, 0))
    in_specs = [
        pl.BlockSpec((None, MLA_HEADS, t_new, 2 * LANES), lambda b, c, pt: (0, 0, b, 0)),
        new(512), new(2 * LANES), new(512), new(512),
        pl.BlockSpec((8, LANES), lambda b, c, pt: (0, 0)),
    ]
    in_specs += [pl.BlockSpec((None, None, PAGE, KV_LORA), page(j)) for j in range(pps)]
    in_specs += [pl.BlockSpec((None, None, MLA_ROPE, PAGE), page(j)) for j in range(pps)]
    in_specs += [pl.BlockSpec((None, None, DIFF_HEADS, 2, DIFF_DH, PAGE), page4(j)) for j in range(pps)]
    in_specs += [pl.BlockSpec((None, None, PAGE * DIFF_HEADS, DIFF_V), page(j)) for j in range(pps)]
    out_specs = (new(MLA_HEADS * LANES), new(512))
    out_shape = (jax.ShapeDtypeStruct((1, tot, MLA_HEADS * LANES), F32),
                 jax.ShapeDtypeStruct((1, tot, 512), F32))
    drows = DIFF_HEADS * 2 * t_new
    scratch = [
        pltpu.VMEM((rows, 2 * LANES), F32), pltpu.VMEM((drows, DIFF_HEADS * LANES), F32),
        pltpu.VMEM((rows, 1), F32), pltpu.VMEM((rows, 1), F32), pltpu.VMEM((rows, KV_LORA), F32),
        pltpu.VMEM((drows, 1), F32), pltpu.VMEM((drows, 1), F32), pltpu.VMEM((drows, DIFF_V), F32),
    ]
    grid_spec = pltpu.PrefetchScalarGridSpec(
        num_scalar_prefetch=1, grid=(nbatch, nc), in_specs=in_specs, out_specs=out_specs, scratch_shapes=scratch)
    return pl.pallas_call(
        functools.partial(_decode_kernel, pps=pps, t_new=t_new, lam_init=lam_init),
        grid_spec=grid_spec, out_shape=out_shape,
        compiler_params=_cp("parallel", "arbitrary"), name="decode_attn",
    )(page_table, qm, dq, kcat, dkn, dvn, lam,
      *([lat_pool] * pps), *([ropeT_pool] * pps), *([dkT_pool] * pps), *([dv_pool] * pps))


def _attn_mix(x, olat_ref, odiff_ref, wuv_ref, gsub_ref, wo_ref, gpost_ref, scale):
    o_mla = _dot(olat_ref[...].astype(BF16), wuv_ref[...])
    od = odiff_ref[...]
    parts = [o_mla]
    for h in range(DIFF_HEADS):
        parts.append(_rms(od[:, h * LANES:(h + 1) * LANES], gsub_ref[...]) * scale)
    mix = _dot(jnp.concatenate(parts, axis=1).astype(BF16), wo_ref[...])
    return x + _rms(mix, gpost_ref[...])


def _gla_mix(x, o_ref, r_ref, gout_ref, wo_ref, gpost_ref):
    o = o_ref[...]
    parts = [_rms(o[:, h * GLA_DV:(h + 1) * GLA_DV], gout_ref[...]) for h in range(GLA_HEADS)]
    y = jnp.concatenate(parts, axis=1) * jax.nn.silu(r_ref[...])
    mix = _dot(y.astype(BF16), wo_ref[...])
    return x + _rms(mix, gpost_ref[...])


FFN_CHUNK = 256
FFN_AHEAD = 2
N_MIX_REFS = {'attn': 6, 'gla': 5}


def _ffn_kernel(x_ref, *refs, period, has_prev, mixer, scale):
    mix_refs, refs = refs[:N_MIX_REFS[mixer]], refs[N_MIX_REFS[mixer]:]
    if has_prev:
        gpre_ref, gpost_ref, wup_ref, wconv_ref, bconv_ref, wdown_ref, prev_ref, o_ref, st_ref, carry = refs
    else:
        gpre_ref, gpost_ref, wup_ref, wconv_ref, bconv_ref, wdown_ref, o_ref, st_ref, carry = refs
    tm = x_ref.shape[0]
    i = pl.program_id(1)
    if mixer == 'attn':
        x = _attn_mix(x_ref[...], *mix_refs, scale)
    else:
        x = _gla_mix(x_ref[...], *mix_refs)
    h = _rms(x, gpre_ref[...]).astype(BF16)
    row = lax.broadcasted_iota(jnp.int32, (tm, FFN_CHUNK), 0)
    t = row & (period - 1) if period < tm else row

    if not has_prev:
        @pl.when(i == 0)
        def _():
            carry[...] = jnp.zeros(carry.shape, F32)

    def up(c):
        return (_dot(h, wup_ref[:, c * FFN_CHUNK:(c + 1) * FFN_CHUNK]),
                _dot(h, wup_ref[:, D_FF + c * FFN_CHUNK:D_FF + (c + 1) * FFN_CHUNK]))

    nchunk = D_FF // FFN_CHUNK
    y = None
    pend = [up(c) for c in range(FFN_AHEAD)]
    for c in range(nchunk):
        sl = slice(c * FFN_CHUNK, (c + 1) * FFN_CHUNK)
        if c + FFN_AHEAD < nchunk:
            pend.append(up(c + FFN_AHEAD))
        a, v = pend.pop(0)
        if has_prev:
            nseq = tm // period
            pr = prev_ref[:, :, sl]
            p0 = jnp.broadcast_to(pr[:, 0:1, :], (nseq, period, FFN_CHUNK)).reshape(tm, FFN_CHUNK)
            p1 = jnp.broadcast_to(pr[:, 1:2, :], (nseq, period, FFN_CHUNK)).reshape(tm, FFN_CHUNK)
            st_ref[:, :, sl] = a.reshape(nseq, period, FFN_CHUNK)[:, period - 2:, :]
        else:
            cr = carry[:, sl]
            p0 = cr[6:7, :]
            p1 = cr[7:8, :]
            carry[:, sl] = a[tm - 8:, :]
            st_ref[:, sl] = a[tm - 2:, :]
        am1 = jnp.where(t == 0, p1, pltpu.roll(a, 1, 0))
        am2 = jnp.where(t == 0, p0, jnp.where(t == 1, p1, pltpu.roll(a, 2, 0)))
        wc = wconv_ref[:, sl]
        ac = bconv_ref[:, sl] + wc[0:1] * am2 + wc[1:2] * am1 + wc[2:3] * a
        gelu = 0.5 * ac * (1.0 + lax.erf(ac * math.sqrt(0.5)))
        z = (gelu * v).astype(BF16)
        yc = _dot(z, wdown_ref[sl, :])
        y = yc if y is None else y + yc
    o_ref[...] = x + _rms(y, gpost_ref[...])


def _ffn(x, mixer, mix_rows, mix_consts, scale, gpre, gpost, w_up, w_conv, b_conv, w_down, prev, tm):
    nb, t, d = x.shape
    has_prev = prev is not None
    row = pl.BlockSpec((None, tm, d), lambda b, i: (b, i, 0))
    rowc = lambda c: pl.BlockSpec((None, tm, c), lambda b, i: (b, i, 0))
    in_specs = [row] + [rowc(a.shape[-1]) for a in mix_rows] + [_const_spec(a.shape) for a in mix_consts]
    assert len(mix_rows) + len(mix_consts) == N_MIX_REFS[mixer]
    in_specs += [_const_spec((1, d)), _const_spec((1, d)), _const_spec(w_up.shape), _const_spec((3, D_FF)),
                 _const_spec((1, D_FF)), _const_spec(w_down.shape)]
    args = [x, *mix_rows, *mix_consts, gpre, gpost, w_up, w_conv, b_conv, w_down]
    if has_prev:
        nseq = prev.shape[0]
        period = t // nseq
        spt = tm // period
        in_specs.append(pl.BlockSpec((spt, 2, D_FF), lambda b, i: (i, 0, 0)))
        args.append(prev)
        st_spec = pl.BlockSpec((spt, 2, D_FF), lambda b, i: (i, 0, 0))
        st_shape = jax.ShapeDtypeStruct((nseq, 2, D_FF), F32)
    else:
        period = t
        st_spec = pl.BlockSpec((None, 2, D_FF), lambda b, i: (b, 0, 0))
        st_shape = jax.ShapeDtypeStruct((nb, 2, D_FF), F32)
    return pl.pallas_call(
        functools.partial(_ffn_kernel, period=period, has_prev=has_prev, mixer=mixer, scale=scale),
        grid=(nb, t // tm), in_specs=in_specs,
        out_specs=(row, st_spec), out_shape=(jax.ShapeDtypeStruct(x.shape, F32), st_shape),
        scratch_shapes=[pltpu.VMEM((8, D_FF), F32)],
        compiler_params=_cp("parallel", "arbitrary"), name="conv_ffn",
    )(*args)


def _gla_in_kernel(x_ref, g_ref, win_ref, wg_ref, bg_ref, q_ref, k_ref, v_ref, r_ref, ga_ref):
    h = _rms(x_ref[...], g_ref[...]).astype(BF16)
    a = _dot(h, win_ref[:, G_A:G_W])
    z = _dot(a.astype(BF16), wg_ref[...]) + bg_ref[...]
    q_ref[...] = _dot(h, win_ref[:, G_Q:G_K]) * (GLA_DK ** -0.5)
    ga_ref[...] = (jnp.minimum(z, 0.0) - jnp.log1p(jnp.exp(-jnp.abs(z)))) * (LOG2E / GLA_TAU)
    k_ref[...] = _dot(h, win_ref[:, G_K:G_V])
    v_ref[...] = _dot(h, win_ref[:, G_V:G_R]).astype(v_ref.dtype)
    r_ref[...] = _dot(h, win_ref[:, G_R:G_A])


def _gla_in(x, g, w, tm):
    nb, t, d = x.shape
    row = lambda c: pl.BlockSpec((None, tm, c), lambda b, i: (b, i, 0))
    sds = lambda c, dt: jax.ShapeDtypeStruct((nb, t, c), dt)
    return pl.pallas_call(
        _gla_in_kernel, grid=(nb, t // tm),
        in_specs=[row(d), _const_spec((1, d)), _const_spec(w['w_in'].shape), _const_spec(w['w_gate'].shape),
                  _const_spec((1, 512))],
        out_specs=(row(512), row(512), row(1024), row(1024), row(512)),
        out_shape=(sds(512, F32), sds(512, F32), sds(1024, BF16), sds(1024, F32), sds(512, F32)),
        compiler_params=_cp("parallel", "parallel"), name="gla_in",
    )(x, g, w['w_in'], w['w_gate'], w['b_gate'])


def _gla_chunk(q_ref, k_ref, v_ref, g_ref, b_ref, base, head, c, state, sub):
    mm = BF16 if c >= 16 else F32
    kcol = slice(head * GLA_DK, (head + 1) * GLA_DK)
    q = q_ref[base:base + c, kcol]
    k = k_ref[base:base + c, kcol]
    v = v_ref[base:base + c, head * GLA_DV:(head + 1) * GLA_DV].astype(mm)
    row = lax.broadcasted_iota(jnp.int32, (c, GLA_DK), 0)
    b = g_ref[base:base + c, kcol]
    step = 1
    while step < c:
        b = b + jnp.where(row >= step, pltpu.roll(b, step, 0), 0.0)
        step *= 2
    b_ref[...] = b
    b_end = b_ref[c - 1:c, :]
    lane_c = lax.broadcasted_iota(jnp.int32, (sub, c), 1)
    rsub = lax.broadcasted_iota(jnp.int32, (sub, c), 0)
    blocks = []
    for i in range(c // sub):
        r0 = i * sub
        qi = q[r0:r0 + sub]
        bi = b[r0:r0 + sub]
        gr = min(sub, 8)
        d_g = [jnp.zeros((gr, c), F32) for _ in range(sub // gr)]
        for s in range(sub):
            r = r0 + s
            ks = k_ref[base + r:base + r + 1, kcol]
            bs = b_ref[r:r + 1, :]
            onehot = (lane_c[0:1] == r).astype(F32)
            for g in range(s // gr, sub // gr):
                rows = slice(g * gr, (g + 1) * gr)
                w = qi[rows] * ks * jnp.exp2(jnp.minimum(bi[rows] - bs, 0.0))
                d_g[g] = d_g[g] + jnp.sum(w, axis=1, keepdims=True) * onehot
        d_i = d_g[0] if len(d_g) == 1 else jnp.concatenate(d_g, axis=0)
        a_i = jnp.where(lane_c - r0 <= rsub, d_i, 0.0)
        if i > 0:
            beta = b_ref[r0 - 1:r0, :]
            qt = (qi * jnp.exp2(bi - beta)).astype(mm)
            kt = (k * jnp.exp2(jnp.minimum(beta - b, 0.0))).astype(mm)
            below = lax.dot_general(qt, kt, NT, preferred_element_type=F32)
            a_i = jnp.where(lane_c < r0, below, a_i)
        blocks.append(a_i)
    a = blocks[0] if len(blocks) == 1 else jnp.concatenate(blocks, axis=0)
    qd = (q * jnp.exp2(b)).astype(mm)
    o = _dot(qd, state.astype(mm)) + _dot(a.astype(mm), v)
    kd = (k * jnp.exp2(b_end - b)).astype(mm)
    rr = lax.broadcasted_iota(jnp.int32, (GLA_DK, GLA_DK), 0)
    cc = lax.broadcasted_iota(jnp.int32, (GLA_DK, GLA_DK), 1)
    decay = jnp.sum(jnp.where(rr == cc, jnp.exp2(b_end), 0.0), axis=1, keepdims=True)
    new_state = decay * state + lax.dot_general(kd, v, TN, preferred_element_type=F32)
    return o, new_state


def _gla_kernel(*refs, chunk, sub, nseq, has_s0):
    if has_s0:
        q_ref, k_ref, v_ref, g_ref, s0_ref, _, o_ref, s_ref, st, b_scr = refs
    else:
        q_ref, k_ref, v_ref, g_ref, _, o_ref, s_ref, st, b_scr = refs

        @pl.when(pl.program_id(1) == 0)
        def _():
            st[...] = jnp.zeros(st.shape, F32)

    rows = q_ref.shape[0] // nseq
    for j in range(nseq):
        states = [s0_ref[j, h] if has_s0 else st[h] for h in range(GLA_HEADS)]
        for n in range(rows // chunk):
            base = j * rows + n * chunk
            for h in range(GLA_HEADS):
                o, states[h] = _gla_chunk(q_ref, k_ref, v_ref, g_ref, b_scr.at[h], base, h, chunk, states[h], sub)
                o_ref[base:base + chunk, h * GLA_DV:(h + 1) * GLA_DV] = o
        for h in range(GLA_HEADS):
            if not has_s0:
                st[h] = states[h]
            s_ref[j, h] = states[h]


def _gla(q, k, v, g, s0, s_buf, layer, chunk, sub, tt, nseq):
    nb, t, _ = q.shape
    has_s0 = s0 is not None
    assert (nb == 1) if has_s0 else (nseq == 1)
    blk = lambda w: pl.BlockSpec((None, tt, w), lambda b, i: (b, i, 0))
    st_blk = (None, nseq, GLA_HEADS, GLA_DK, GLA_DV)
    in_specs = [blk(GLA_HEADS * GLA_DK), blk(GLA_HEADS * GLA_DK), blk(GLA_HEADS * GLA_DV), blk(GLA_HEADS * GLA_DK)]
    args = [q, k, v, g]
    if has_s0:
        in_specs.append(pl.BlockSpec(st_blk, lambda b, i: (layer, i, 0, 0, 0)))
        args.append(s0)
        s_spec = pl.BlockSpec(st_blk, lambda b, i: (layer, i, 0, 0, 0))
        assert s_buf.shape[1] == (t // tt) * nseq
    else:
        s_spec = pl.BlockSpec(st_blk, lambda b, i: (layer, b, 0, 0, 0))
        assert s_buf.shape[1] == nb
    in_specs.append(pl.BlockSpec(memory_space=pl.ANY))
    args.append(s_buf)
    return pl.pallas_call(
        functools.partial(_gla_kernel, chunk=chunk, sub=sub, nseq=nseq, has_s0=has_s0),
        grid=(nb, t // tt), in_specs=in_specs, out_specs=(blk(GLA_HEADS * GLA_DV), s_spec),
        out_shape=(jax.ShapeDtypeStruct((nb, t, GLA_HEADS * GLA_DV), F32),
                   jax.ShapeDtypeStruct(s_buf.shape, F32)),
        input_output_aliases={len(args) - 1: 1},
        scratch_shapes=[pltpu.VMEM((GLA_HEADS, GLA_DK, GLA_DV), F32), pltpu.VMEM((GLA_HEADS, chunk, GLA_DK), F32)],
        compiler_params=_cp("parallel", "arbitrary"), name="gla_core",
    )(*args)


def _prep_attn(i, W):
    w_in = W['w_attn_in'][i]
    cq, ckv, kr, dq, dk, dv = jnp.split(w_in, [256, 384, 416, 928, 1440], axis=1)
    w_in2 = jnp.concatenate([cq, ckv, dq, dk, dv, kr, jnp.zeros((D_MODEL, A_W - A_KR - MLA_ROPE), F32)], axis=1)
    wq = W['w_uq'][i].reshape(Q_LORA, MLA_HEADS, MLA_NOPE + MLA_ROPE)
    w_nope = wq[:, :, :MLA_NOPE].reshape(Q_LORA, MLA_HEADS * MLA_NOPE)
    w_rope = jnp.pad(wq[:, :, MLA_NOPE:], ((0, 0), (0, 0), (0, LANES - MLA_ROPE))).reshape(Q_LORA, MLA_HEADS * LANES)
    eye = jnp.eye(MLA_HEADS, dtype=F32)
    w_ukbd = jnp.einsum('chn,hg->hngc', W['w_uk'][i], eye).reshape(MLA_HEADS * MLA_NOPE, MLA_HEADS * KV_LORA)
    w_uvbd = jnp.einsum('chv,hg->hcgv', W['w_uv'][i], eye).reshape(MLA_HEADS * KV_LORA, MLA_HEADS * MLA_V)
    lam = jnp.zeros((8, LANES), F32)
    for r, name in enumerate(('lambda_q1', 'lambda_k1', 'lambda_q2', 'lambda_k2')):
        lam = lam.at[r, :DIFF_DH].set(W[name][i])
    return dict(
        w_in=w_in2.astype(BF16), g_q=W['g_q_norm'][i][None], w_uq=jnp.concatenate([w_nope, w_rope], 1).astype(BF16),
        g_kv=W['g_kv_norm'][i][None], w_ukbd=w_ukbd.astype(BF16), w_uvbd=w_uvbd.astype(BF16),
        g_sub=W['g_diff_subln'][i][None], w_out=W['w_attn_out'][i].astype(BF16), lam=lam)


def _prep_gla(i, W):
    w_in = W['w_gla_in'][i]
    q, k, v, a, r = jnp.split(w_in, [512, 1024, 2048, 2064], axis=1)
    w_in2 = jnp.concatenate([q, k, v, r, a, jnp.zeros((D_MODEL, G_W - G_A - GLA_RANK), F32)], axis=1)
    w_gate = jnp.pad(W['w_gla_gate'][i], ((0, LANES - GLA_RANK), (0, 0)))
    return dict(w_in=w_in2.astype(BF16), w_gate=w_gate.astype(BF16), b_gate=W['b_gla_gate'][i][None],
                g_out=W['g_gla_out'][i][None], w_out=W['w_gla_out'][i].astype(BF16))


def _tile(t, pref):
    return pref if t % pref == 0 else t


def _trunk(x, tab, W, prep, sample):
    nb, t, d = x.shape
    tm = _tile(t, 512)
    act = F32 if sample is not None else BF16
    n_attn, n_gla = (DEPTH + 1) // 2, DEPTH // 2
    caches = (jnp.zeros((n_attn, nb, t, KV_LORA), F32), jnp.zeros((n_attn, nb, t, MLA_ROPE), F32),
              jnp.zeros((n_attn, nb, 512, t), F32), jnp.zeros((n_attn, nb, t * DIFF_HEADS, DIFF_V), F32))
    gla = jnp.zeros((n_gla, nb if sample is None else sample['nseq'], GLA_HEADS, GLA_DK, GLA_DV), F32)
    conv = []
    for l in range(DEPTH):
        i = l // 2
        g_pre = W['g_mix_pre'][l][None]
        g_post = W['g_mix_post'][l][None]
        w = prep[l]
        if l % 2 == 0:
            lam_init = 0.8 - 0.6 * math.exp(-0.3 * l)
            qm, kcat, dq, dkb, dvb, *caches = _attn_in(x, g_pre, w, tab, act, tm, i, caches)
            if sample is None:
                olat, odiff = _flash(qm, kcat, dq, dkb, dvb, w['lam'], lam_init, _tile(t, 256))
            else:
                olat, odiff = _decode(i, qm, dq, kcat, dkb, dvb, w['lam'], lam_init, sample['lat'], sample['ropeT'],
                                      sample['dkT'], sample['dv'], sample['page_table'], sample['pps'])
            mix = ('attn', (olat, odiff), (w['w_uvbd'], w['g_sub'], w['w_out'], g_post), 1.0 - lam_init)
        else:
            q, k, v, r, ga = _gla_in(x, g_pre, w, tm)
            if sample is None:
                o, gla = _gla(q, k, v, ga, None, gla, i, 64, 16, _tile(t, 256), 1)
            else:
                t_new = t // sample['nseq']
                per_step = 8 if sample['nseq'] % 8 == 0 else 1
                o, gla = _gla(q, k, v, ga, sample['state_gla'], gla, i, t_new, t_new, per_step * t_new, per_step)
            mix = ('gla', (o, r), (w['g_out'], w['w_out'], g_post), None)
        prev = None if sample is None else sample['state_ffn_conv'][l]
        x, c = _ffn(x, *mix, W['g_ffn_pre'][l][None], W['g_ffn_post'][l][None], prep['w_up'][l], W['w_ffn_conv'][l],
                    W['b_ffn_conv'][l][None], prep['w_down'][l], prev, _tile(t, 256))
        conv.append(c)
    return (x,) + tuple(caches) + (gla, jnp.stack(conv))


def kernel(x_prompt, x_sample, cache_mla_latent, cache_mla_rope, cache_diff_k, cache_diff_v, state_gla, state_ffn_conv, page_table, g_mix_pre, g_mix_post, g_ffn_pre, g_ffn_post, w_attn_in, g_q_norm, w_uq, g_kv_norm, w_uk, w_uv, lambda_q1, lambda_k1, lambda_q2, lambda_k2, g_diff_subln, w_attn_out, w_gla_in, w_gla_gate, b_gla_gate, g_gla_out, w_gla_out, w_ffn_up, w_ffn_conv, b_ffn_conv, w_ffn_down):
    W = dict(g_mix_pre=g_mix_pre, g_mix_post=g_mix_post, g_ffn_pre=g_ffn_pre, g_ffn_post=g_ffn_post,
             w_attn_in=w_attn_in, g_q_norm=g_q_norm, w_uq=w_uq, g_kv_norm=g_kv_norm, w_uk=w_uk, w_uv=w_uv,
             lambda_q1=lambda_q1, lambda_k1=lambda_k1, lambda_q2=lambda_q2, lambda_k2=lambda_k2,
             g_diff_subln=g_diff_subln, w_attn_out=w_attn_out, w_gla_in=w_gla_in, w_gla_gate=w_gla_gate,
             b_gla_gate=b_gla_gate, g_gla_out=g_gla_out, w_gla_out=w_gla_out, w_ffn_conv=w_ffn_conv,
             b_ffn_conv=b_ffn_conv)
    prep = {l: (_prep_attn(l // 2, W) if l % 2 == 0 else _prep_gla(l // 2, W)) for l in range(DEPTH)}
    prep['w_up'] = w_ffn_up.astype(BF16)
    prep['w_down'] = w_ffn_down.astype(BF16)

    bp, sp, d = x_prompt.shape
    bs, ts, _ = x_sample.shape
    n_pages = page_table.shape[1]
    past = n_pages * PAGE

    yp, p_lat, p_rope, p_k, p_v, p_gla, p_conv = _trunk(x_prompt, _rope_tables(jnp.arange(sp)), W, prep, None)

    la, n_pool = cache_mla_latent.shape[:2]
    sample = dict(
        lat=cache_mla_latent,
        ropeT=jnp.swapaxes(cache_mla_rope, 2, 3),
        dkT=jnp.transpose(cache_diff_k, (0, 1, 3, 4, 5, 2)),
        dv=cache_diff_v.reshape(la, n_pool, PAGE * DIFF_HEADS, DIFF_V),
        page_table=page_table, pps=next(p for p in (32, 16, 8, 4, 2, 1) if n_pages % p == 0), nseq=bs,
        state_gla=state_gla, state_ffn_conv=state_ffn_conv)
    tab_s = jnp.tile(_rope_tables(past + jnp.arange(ts)), (bs, 1))
    ys, s_lat, s_rope, s_k, s_v, s_gla, s_conv = _trunk(x_sample.reshape(1, bs * ts, d), tab_s, W, prep, sample)

    def k6(kt, b, t):
        nb = kt.shape[1]
        kt = kt.reshape(kt.shape[0], nb, DIFF_HEADS, 2, DIFF_DH, b // nb, t)
        return jnp.transpose(kt, (0, 1, 5, 6, 2, 3, 4)).reshape(kt.shape[0], b, t, DIFF_HEADS, 2, DIFF_DH)

    def v5(a, b, t):
        return a.reshape(a.shape[0], b, t, DIFF_HEADS, DIFF_V)

    return (yp, ys.reshape(bs, ts, d),
            p_lat, p_rope, k6(p_k, bp, sp), v5(p_v, bp, sp), p_gla, p_conv,
            s_lat.reshape(-1, bs, ts, KV_LORA), s_rope.reshape(-1, bs, ts, MLA_ROPE), k6(s_k, bs, ts), v5(s_v, bs, ts),
            s_gla, s_conv)
```
